```python
import functools
import jax, jax.numpy as jnp
from jax import lax
import numpy as np


D_MODEL = 1024
BATCH = 2
SEQ = 16384
DEPTH = 4
DEC_BATCH = 8
DEC_SEQ = 32
PAST_LEN = 4096

CHUNK = 64
N_META = 16
N_A_LAYERS = DEPTH // 2
N_B_LAYERS = DEPTH - N_A_LAYERS
N_DENSE = (DEPTH + 1) // 2
N_MOE = DEPTH // 2
RET_HEADS = 4
RET_DK = D_MODEL // RET_HEADS
RET_DV = 2 * RET_DK
RET_IN = RET_HEADS * (2 * RET_DK + 2 * RET_DV)
RET_THETA_BASE = 10000.0
SB_HEADS = 8
SB_KV_HEADS = 4
SB_GROUP = SB_HEADS // SB_KV_HEADS
SB_DH = D_MODEL // SB_HEADS
Q_BLOCK = 128
KB = 64
D_FF = 2816
N_EXP = 8
TOP_K = 2
D_FF_EXP = 1024
EPS = 1e-6

kernel_name = 'stream_retention_stickbreak_yoco'


def rms_norm(x, g):
    xf = x.astype(jnp.float32)
    y = xf * lax.rsqrt(jnp.mean(xf * xf, axis=-1, keepdims=True) + EPS)
    return (y * g.astype(jnp.float32)).astype(x.dtype)


def rope(x, pos):
    half = x.shape[-1] // 2
    inv_freq = 1.0 / (RET_THETA_BASE ** jnp.linspace(0.0, 1.0, half, dtype=jnp.float32))
    ang = pos.astype(jnp.float32)[:, None] * inv_freq[None, :]
    cos = jnp.cos(ang)[None, :, None, :]
    sin = jnp.sin(ang)[None, :, None, :]
    xf = x.astype(jnp.float32)
    x1, x2 = xf[..., :half], xf[..., half:]
    return jnp.concatenate([x1 * cos - x2 * sin, x2 * cos + x1 * sin], axis=-1)


def ret_log_decay():
    return jnp.log(1.0 - 2.0 ** (-5.0 - jnp.arange(RET_HEADS, dtype=jnp.float32)))


def ret_project(xn, w_in, pos):
    b, l, _ = xn.shape
    p = xn @ w_in
    q, k, v, g = jnp.split(p, [RET_HEADS * RET_DK, 2 * RET_HEADS * RET_DK,
                               2 * RET_HEADS * RET_DK + RET_HEADS * RET_DV], axis=-1)
    q = rope(q.reshape(b, l, RET_HEADS, RET_DK), pos).transpose(0, 2, 1, 3)
    k = (rope(k.reshape(b, l, RET_HEADS, RET_DK), pos) * RET_DK ** -0.5).transpose(0, 2, 1, 3)
    v = v.reshape(b, l, RET_HEADS, RET_DV).astype(jnp.float32).transpose(0, 2, 1, 3)
    return q, k, v, g


def ret_block(q, k, v, S):
    L = q.shape[2]
    lg = ret_log_decay()[:, None]
    i = jnp.arange(L, dtype=jnp.float32)
    diff = i[:, None] - i[None, :]
    dmat = jnp.where(diff >= 0, jnp.exp(jnp.maximum(diff, 0.0)[None] * lg[:, :, None]), 0.0)
    q_dec = jnp.exp((i + 1.0)[None] * lg)
    k_dec = jnp.exp((L - 1.0 - i)[None] * lg)
    s_dec = jnp.exp(L * lg)[..., None]
    scores = jnp.einsum('bhid,bhjd->bhij', q, k) * dmat
    o = (jnp.einsum('bhij,bhje->bhie', scores, v)
         + jnp.einsum('bhid,bhde->bhie', q * q_dec[..., None], S))
    S_new = s_dec * S + jnp.einsum('bhjd,bhje->bhde', k * k_dec[..., None], v)
    return o, S_new


def ret_scan(q, k, v, S):
    b, h, L, _ = q.shape
    n = L // CHUNK

    def blocks(t):
        return t.reshape(b, h, n, CHUNK, t.shape[-1]).transpose(2, 0, 1, 3, 4)

    def step(s, blk):
        o, s = ret_block(blk[0], blk[1], blk[2], s)
        return s, o

    S, o = lax.scan(step, S, (blocks(q), blocks(k), blocks(v)))
    return o.transpose(1, 2, 0, 3, 4).reshape(b, h, L, -1), S


def retention_prompt(q, k, v):
    b = q.shape[0]
    S0 = jnp.zeros((b, RET_HEADS, RET_DK, RET_DV), jnp.float32)
    o_m, S = ret_block(q[:, :, :N_META], k[:, :, :N_META], v[:, :, :N_META], S0)
    o_f, S = ret_scan(q[:, :, N_META:], k[:, :, N_META:], v[:, :, N_META:], S)
    return jnp.concatenate([o_m, o_f], axis=2), S


def ret_output(o, g, w_out):
    b, h, L, _ = o.shape
    o = o * lax.rsqrt(jnp.mean(o * o, axis=-1, keepdims=True) + EPS)
    o = o.transpose(0, 2, 1, 3).reshape(b, L, h * RET_DV).astype(g.dtype)
    return (jax.nn.silu(g) * o) @ w_out


def swiglu(x, w_in, w_out):
    gate, up = jnp.split(x @ w_in, 2, axis=-1)
    return (jax.nn.silu(gate) * up) @ w_out


def moe_swiglu(x, w_r, b_r, w_in, w_out):
    logits = jnp.einsum('bld,de->ble', x, w_r).astype(jnp.float32) + b_r.astype(jnp.float32)
    top_v, top_i = lax.top_k(logits, TOP_K)
    gates = jax.nn.softmax(top_v, axis=-1)
    combine = jnp.sum(jax.nn.one_hot(top_i, N_EXP, dtype=jnp.float32) * gates[..., None], axis=-2)
    y = jnp.zeros_like(x)
    for e in range(N_EXP):
        y = y + combine[..., e:e + 1].astype(x.dtype) * swiglu(x, w_in[e], w_out[e])
    return y


def shared_kv(h, g_kv, w_kv, k_norm):
    b, L, _ = h.shape
    k, v = jnp.split(rms_norm(h, g_kv) @ w_kv, 2, axis=-1)
    k = rms_norm(k.reshape(b, L, SB_KV_HEADS, SB_DH), k_norm)
    return k, v.reshape(b, L, SB_KV_HEADS, SB_DH)


def pad_keys(k, v):
    n = k.shape[1]
    pad = (-n) % KB
    kT = jnp.pad(k, ((0, 0), (pad, 0), (0, 0), (0, 0))).transpose(0, 2, 1, 3)
    vT = jnp.pad(v, ((0, 0), (pad, 0), (0, 0), (0, 0))).transpose(0, 2, 1, 3)
    k_pos = jnp.concatenate([-jnp.ones((pad,), jnp.int32), jnp.arange(n, dtype=jnp.int32)])
    return kT, vT, k_pos, pad


def sb_queries(h, g_mix, w_q, q_norm):
    b, L, _ = h.shape
    q = rms_norm((rms_norm(h, g_mix) @ w_q).reshape(b, L, SB_HEADS, SB_DH), q_norm)
    return q.reshape(b, L, SB_KV_HEADS, SB_GROUP, SB_DH)


def stick_breaking(q, k, v, q_pos, k_pos):
    b, h, g, nq, _ = q.shape
    n = k.shape[2]
    nk = n // KB
    z = jnp.einsum('bhgqd,bhnd->bhgqn', q, k, preferred_element_type=jnp.float32) * SB_DH ** -0.5
    valid = (k_pos[None, :] >= 0) & (k_pos[None, :] < q_pos[:, None])
    lsm = jnp.where(valid, -jax.nn.softplus(z), 0.0).reshape(b, h, g, nq, nk, KB)
    tri = jnp.tril(jnp.ones((KB, KB), jnp.float32))
    within = jnp.einsum('bhgqkn,nm->bhgqkm', lsm, tri)
    later = jnp.einsum('bhgqk,kj->bhgqj', within[..., 0],
                       jnp.tril(jnp.ones((nk, nk), jnp.float32), -1))
    c = (within + later[..., None]).reshape(b, h, g, nq, n)
    a = jnp.where(valid, jnp.exp(z + c), 0.0)
    return jnp.einsum('bhgqn,bhnd->bhgqd', a.astype(v.dtype), v)


def setup_inputs(seed: int = 0) -> dict:
    key = jax.random.key(seed)
    ks = jax.random.split(key, 24)
    f32 = jnp.float32
    out_scale = (2.0 * DEPTH) ** -0.5

    def nrm(k, shape, scale):
        return jax.random.normal(k, shape, f32) * scale

    return {
        'x_prompt': nrm(ks[0], (BATCH, SEQ, D_MODEL), 1.0),
        'x_sample': nrm(ks[1], (DEC_BATCH, DEC_SEQ, D_MODEL), 1.0),
        'state_ret': nrm(ks[2], (N_A_LAYERS, DEC_BATCH, RET_HEADS, RET_DK, RET_DV), 0.5),
        'cache_k': nrm(ks[3], (DEC_BATCH, N_META + PAST_LEN, SB_KV_HEADS, SB_DH), 1.0),
        'cache_v': nrm(ks[4], (DEC_BATCH, N_META + PAST_LEN, SB_KV_HEADS, SB_DH), 1.0),
        'meta_tokens': nrm(ks[5], (N_META, D_MODEL), 1.0),
        'norm_mix': 1.0 + nrm(ks[6], (DEPTH, D_MODEL), 0.02),
        'norm_ffn': 1.0 + nrm(ks[7], (DEPTH, D_MODEL), 0.02),
        'w_ret_in': nrm(ks[8], (N_A_LAYERS, D_MODEL, RET_IN), D_MODEL ** -0.5),
        'w_ret_out': nrm(ks[9], (N_A_LAYERS, RET_HEADS * RET_DV, D_MODEL), (RET_HEADS * RET_DV) ** -0.5 * out_scale),
        'norm_kv': 1.0 + nrm(ks[10], (D_MODEL,), 0.02),
        'w_kv': nrm(ks[11], (D_MODEL, 2 * SB_KV_HEADS * SB_DH), D_MODEL ** -0.5),
        'k_norm': 1.0 + nrm(ks[12], (SB_DH,), 0.02),
        'w_sb_q': nrm(ks[13], (N_B_LAYERS, D_MODEL, SB_HEADS * SB_DH), D_MODEL ** -0.5),
        'q_norm': 1.0 + nrm(ks[14], (N_B_LAYERS, SB_DH), 0.02),
        'w_sb_o': nrm(ks[15], (N_B_LAYERS, SB_HEADS * SB_DH, D_MODEL), (SB_HEADS * SB_DH) ** -0.5 * out_scale),
        'w_ffn_in': nrm(ks[16], (N_DENSE, D_MODEL, 2 * D_FF), D_MODEL ** -0.5),
        'w_ffn_out': nrm(ks[17], (N_DENSE, D_FF, D_MODEL), D_FF ** -0.5 * out_scale),
        'w_router': nrm(ks[18], (N_MOE, D_MODEL, N_EXP), D_MODEL ** -0.5),
        'b_router': nrm(ks[19], (N_MOE, N_EXP), 0.01),
        'w_exp_in': nrm(ks[20], (N_MOE, N_EXP, D_MODEL, 2 * D_FF_EXP), D_MODEL ** -0.5),
        'w_exp_out': nrm(ks[21], (N_MOE, N_EXP, D_FF_EXP, D_MODEL), D_FF_EXP ** -0.5 * out_scale),
    }


def reference(x_prompt, x_sample, state_ret, cache_k, cache_v, meta_tokens, norm_mix, norm_ffn,
              w_ret_in, w_ret_out, norm_kv, w_kv, k_norm, w_sb_q, q_norm, w_sb_o,
              w_ffn_in, w_ffn_out, w_router, b_router, w_exp_in, w_exp_out):

    def channel(l, h):
        xn = rms_norm(h, norm_ffn[l])
        if l % 2 == 0:
            return h + swiglu(xn, w_ffn_in[l // 2], w_ffn_out[l // 2])
        return h + moe_swiglu(xn, w_router[l // 2], b_router[l // 2], w_exp_in[l // 2], w_exp_out[l // 2])

    def mixer_a(l, h, pos, core):
        q, k, v, g = ret_project(rms_norm(h, norm_mix[l]), w_ret_in[l], pos)
        o, s = core(q, k, v)
        return h + ret_output(o, g, w_ret_out[l]), s

    def mixer_b(l, h, kT, vT, k_pos, pad, p0):
        b, L, _ = h.shape
        j = l - N_A_LAYERS
        q = sb_queries(h, norm_mix[l], w_sb_q[j], q_norm[j]).transpose(0, 2, 3, 1, 4)
        n_all = kT.shape[2]
        outs = []
        for s in range(0, L, Q_BLOCK):
            e = min(s + Q_BLOCK, L)
            n_keys = min(-(-(pad + p0 + e) // KB) * KB, n_all)
            q_pos = p0 + jnp.arange(s, e, dtype=jnp.int32)
            outs.append(stick_breaking(q[:, :, :, s:e], kT[:, :, :n_keys], vT[:, :, :n_keys],
                                       q_pos, k_pos[:n_keys]))
        o = jnp.concatenate(outs, axis=3).transpose(0, 3, 1, 2, 4).reshape(b, L, SB_HEADS * SB_DH)
        return h + o @ w_sb_o[j]

    bp, seq = x_prompt.shape[0], x_prompt.shape[1]
    meta = jnp.broadcast_to(meta_tokens.astype(x_prompt.dtype)[None], (bp, N_META, D_MODEL))
    h = jnp.concatenate([meta, x_prompt], axis=1)
    pos_p = jnp.arange(N_META + seq)
    ret_p = []
    for l in range(DEPTH):
        if l < N_A_LAYERS:
            h, s = mixer_a(l, h, pos_p, retention_prompt)
            ret_p.append(s.astype(x_prompt.dtype))
        else:
            if l == N_A_LAYERS:
                k_p, v_p = shared_kv(h, norm_kv, w_kv, k_norm)
                kT_p, vT_p, kpos_p, pad_p = pad_keys(k_p, v_p)
                h = h[:, N_META:]
            h = mixer_b(l, h, kT_p, vT_p, kpos_p, pad_p, N_META)
        h = channel(l, h)
    y_prompt = h

    ds = x_sample.shape[1]
    p0_s = N_META + PAST_LEN
    pos_s = p0_s + jnp.arange(ds)
    h = x_sample
    ret_s = []
    for l in range(DEPTH):
        if l < N_A_LAYERS:
            core = functools.partial(ret_block, S=state_ret[l].astype(jnp.float32))
            h, s = mixer_a(l, h, pos_s, core)
            ret_s.append(s.astype(state_ret.dtype))
        else:
            if l == N_A_LAYERS:
                k_s, v_s = shared_kv(h, norm_kv, w_kv, k_norm)
                kT_s, vT_s, kpos_s, pad_s = pad_keys(
                    jnp.concatenate([cache_k, k_s.astype(cache_k.dtype)], axis=1),
                    jnp.concatenate([cache_v, v_s.astype(cache_v.dtype)], axis=1))
            h = mixer_b(l, h, kT_s, vT_s, kpos_s, pad_s, p0_s)
        h = channel(l, h)
    y_sample = h

    return (y_prompt, y_sample, jnp.stack(ret_p), k_p, v_p, jnp.stack(ret_s), k_s, v_s)
```

```python
import functools
import math

import jax
import jax.numpy as jnp
from jax import lax
from jax.experimental import pallas as pl
from jax.experimental.pallas import tpu as pltpu

D_MODEL = 1024
N_META = 16
CHUNK = 64
RET_HEADS = 4
RET_DK = 256
RET_DV = 512
RET_IN = RET_HEADS * (2 * RET_DK + 2 * RET_DV)
RET_THETA_BASE = 10000.0
SB_HEADS = 8
SB_KV_HEADS = 4
SB_GROUP = SB_HEADS // SB_KV_HEADS
SB_DH = 128
N_EXP = 8
EPS = 1e-6
LOG2E = 1.4426950408889634

LANES = 128
BF16_ROWS = 16
VMEM_LIMIT = 56 * 1024 * 1024

F32 = jnp.float32
BF16 = jnp.bfloat16


def _params(n_axes):
    return pltpu.CompilerParams(dimension_semantics=("arbitrary",) * n_axes,
                                vmem_limit_bytes=VMEM_LIMIT)


def _tile(n, pref, mult=BF16_ROWS):
    if n <= pref:
        return n
    t = (pref // mult) * mult
    while t >= mult:
        if n % t == 0:
            return t
        t -= mult
    return n


def _dot(a, b):
    return jnp.dot(a, b, preferred_element_type=F32)


def _rms(x, g):
    return (x * lax.rsqrt(jnp.mean(x * x, axis=-1, keepdims=True) + EPS)) * g


def _rmsnorm_kernel(x_ref, g_ref, o_ref):
    o_ref[...] = _rms(x_ref[...], g_ref[...]).astype(o_ref.dtype)


def rmsnorm(x, g, tm_pref=1024):
    m, d = x.shape
    tm = _tile(m, tm_pref)
    return pl.pallas_call(
        _rmsnorm_kernel,
        grid=(m // tm,),
        in_specs=[pl.BlockSpec((tm, d), lambda i: (i, 0)),
                  pl.BlockSpec((1, d), lambda i: (0, 0))],
        out_specs=pl.BlockSpec((tm, d), lambda i: (i, 0)),
        out_shape=jax.ShapeDtypeStruct((m, d), BF16),
        compiler_params=_params(1),
        name="rmsnorm",
    )(x, g.reshape(1, d))


def _mm_kernel(x_ref, w_ref, o_ref):
    o_ref[...] = _dot(x_ref[...], w_ref[...]).astype(o_ref.dtype)


def matmul(x, w, out_dtype=F32, tm_pref=1024, tn_pref=1024):
    m, k = x.shape
    n = w.shape[1]
    tm = _tile(m, tm_pref)
    tn = _tile(n, tn_pref, LANES)
    return pl.pallas_call(
        _mm_kernel,
        grid=(n // tn, m // tm),
        in_specs=[pl.BlockSpec((tm, k), lambda j, i: (i, 0)),
                  pl.BlockSpec((k, tn), lambda j, i: (0, j))],
        out_specs=pl.BlockSpec((tm, tn), lambda j, i: (i, j)),
        out_shape=jax.ShapeDtypeStruct((m, n), out_dtype),
        compiler_params=_params(2),
        name="matmul",
    )(x, w)


def _swiglu_kernel(x_ref, wg_ref, wu_ref, o_ref):
    x = x_ref[...]
    gate = _dot(x, wg_ref[...])
    up = _dot(x, wu_ref[...])
    o_ref[...] = (gate * jax.nn.sigmoid(gate) * up).astype(o_ref.dtype)


def swiglu_in(x, w_in, tm_pref=512, tn_pref=1408):
    m, k = x.shape
    f = w_in.shape[1] // 2
    tm = _tile(m, tm_pref)
    tn = _tile(f, tn_pref, LANES)
    nj = f // tn
    return pl.pallas_call(
        _swiglu_kernel,
        grid=(nj, m // tm),
        in_specs=[pl.BlockSpec((tm, k), lambda j, i: (i, 0)),
                  pl.BlockSpec((k, tn), lambda j, i: (0, j)),
                  pl.BlockSpec((k, tn), lambda j, i: (0, j + nj))],
        out_specs=pl.BlockSpec((tm, tn), lambda j, i: (i, j)),
        out_shape=jax.ShapeDtypeStruct((m, f), BF16),
        compiler_params=_params(2),
        name="swiglu_in",
    )(x, w_in, w_in)


def _out_kernel(x_ref, w_ref, h_ref, *rest, n_norm):
    g_refs = rest[:n_norm]
    ho_ref = rest[n_norm]
    xn_refs = rest[n_norm + 1:]
    h = h_ref[...] + _dot(x_ref[...], w_ref[...])
    ho_ref[...] = h
    if n_norm:
        y = h * lax.rsqrt(jnp.mean(h * h, axis=-1, keepdims=True) + EPS)
        for g_ref, xn_ref in zip(g_refs, xn_refs):
            xn_ref[...] = (y * g_ref[...]).astype(xn_ref.dtype)


def out_proj(x, w, h, gains=(), tm_pref=512):
    m, k = x.shape
    d = w.shape[1]
    tm = _tile(m, tm_pref)
    n_norm = len(gains)
    row = pl.BlockSpec((tm, d), lambda i: (i, 0))
    gspec = pl.BlockSpec((1, d), lambda i: (0, 0))
    outs = pl.pallas_call(
        functools.partial(_out_kernel, n_norm=n_norm),
        grid=(m // tm,),
        in_specs=[pl.BlockSpec((tm, k), lambda i: (i, 0)),
                  pl.BlockSpec((k, d), lambda i: (0, 0)),
                  row] + [gspec] * n_norm,
        out_specs=[row] * (1 + n_norm),
        out_shape=[jax.ShapeDtypeStruct((m, d), F32)]
        + [jax.ShapeDtypeStruct((m, d), BF16)] * n_norm,
        compiler_params=_params(1),
        name="out_proj",
    )(x, w, h, *[g.reshape(1, d) for g in gains])
    return outs[0], tuple(outs[1:])


def _head_norm(y, g, n_heads):
    parts = []
    for hd in range(n_heads):
        yh = y[:, hd * SB_DH:(hd + 1) * SB_DH]
        parts.append(_rms(yh, g))
    return parts


def _qproj_kernel(x_ref, w_ref, g_ref, o_ref):
    y = _dot(x_ref[...], w_ref[...])
    g = g_ref[...]
    for hd, yh in enumerate(_head_norm(y, g, SB_HEADS)):
        o_ref[:, hd * SB_DH:(hd + 1) * SB_DH] = (yh * (SB_DH ** -0.5 * LOG2E)).astype(o_ref.dtype)


def q_proj(x, w, g, tm_pref=512):
    m, k = x.shape
    n = w.shape[1]
    tm = _tile(m, tm_pref)
    return pl.pallas_call(
        _qproj_kernel,
        grid=(m // tm,),
        in_specs=[pl.BlockSpec((tm, k), lambda i: (i, 0)),
                  pl.BlockSpec((k, n), lambda i: (0, 0)),
                  pl.BlockSpec((1, SB_DH), lambda i: (0, 0))],
        out_specs=pl.BlockSpec((tm, n), lambda i: (i, 0)),
        out_shape=jax.ShapeDtypeStruct((m, n), BF16),
        compiler_params=_params(1),
        name="q_proj",
    )(x, w, g.reshape(1, SB_DH))


def _kvproj_kernel(x_ref, w_ref, g_ref, k_ref, v_ref):
    y = _dot(x_ref[...], w_ref[...])
    nk = SB_KV_HEADS * SB_DH
    g = g_ref[...]
    for hd, yh in enumerate(_head_norm(y[:, :nk], g, SB_KV_HEADS)):
        k_ref[:, hd * SB_DH:(hd + 1) * SB_DH] = yh
    v_ref[...] = y[:, nk:]


def kv_proj(x, w, g, tm_pref=512):
    m, k = x.shape
    n = w.shape[1]
    nk = n // 2
    tm = _tile(m, tm_pref)
    return pl.pallas_call(
        _kvproj_kernel,
        grid=(m // tm,),
        in_specs=[pl.BlockSpec((tm, k), lambda i: (i, 0)),
                  pl.BlockSpec((k, n), lambda i: (0, 0)),
                  pl.BlockSpec((1, SB_DH), lambda i: (0, 0))],
        out_specs=[pl.BlockSpec((tm, nk), lambda i: (i, 0))] * 2,
        out_shape=[jax.ShapeDtypeStruct((m, nk), F32)] * 2,
        compiler_params=_params(1),
        name="kv_proj",
    )(x, w, g.reshape(1, SB_DH))


def _rope(x, cos, sin):
    half = RET_DK // 2
    x1, x2 = x[:, :half], x[:, half:]
    return jnp.concatenate([x1 * cos - x2 * sin, x2 * cos + x1 * sin], axis=1)


def _ret_kernel(pq_ref, pk_ref, pv_ref, pg_ref, cos_ref, sin_ref, dmat_ref, qdec_ref, kdec_ref,
                sdec_ref, s0_ref, o_ref, sout_ref, s_scr, *, chunk, n_sub):
    step = pl.program_id(2)

    @pl.when(step == 0)
    def _():
        s_scr[...] = s0_ref[...]

    dmat = dmat_ref[...]
    qdec = qdec_ref[...]
    kdec = kdec_ref[...]
    sdec = sdec_ref[...]
    for c in range(n_sub):
        rows = pl.ds(c * chunk, chunk)
        cos = cos_ref[rows, :]
        sin = sin_ref[rows, :]
        q = _rope(pq_ref[rows, :], cos, sin)
        k = _rope(pk_ref[rows, :], cos, sin) * (RET_DK ** -0.5)
        v = pv_ref[rows, :].astype(BF16)
        s = s_scr[...]
        scores = lax.dot_general(q.astype(BF16), k.astype(BF16), (((1,), (1,)), ((), ())),
                                 preferred_element_type=F32) * dmat
        o = _dot(scores.astype(BF16), v) + _dot((q * qdec).astype(BF16), s.astype(BF16))
        kd_t = (k * kdec).T.astype(BF16)
        s_scr[...] = sdec * s + _dot(kd_t, v)
        o = o * lax.rsqrt(jnp.mean(o * o, axis=-1, keepdims=True) + EPS)
        g = pg_ref[rows, :]
        o_ref[rows, :] = (g * jax.nn.sigmoid(g) * o).astype(o_ref.dtype)

    @pl.when(step == pl.num_programs(2) - 1)
    def _():
        sout_ref[...] = s_scr[...]


def _ret_tables(chunk):
    lg = jnp.log(1.0 - 2.0 ** (-5.0 - jnp.arange(RET_HEADS, dtype=F32)))[:, None]
    i = jnp.arange(chunk, dtype=F32)
    diff = i[:, None] - i[None, :]
    dmat = jnp.where(diff >= 0, jnp.exp(jnp.maximum(diff, 0.0)[None] * lg[:, :, None]), 0.0)
    q_dec = jnp.exp((i + 1.0)[None] * lg)
    k_dec = jnp.exp((chunk - 1.0 - i)[None] * lg)
    s_dec = jnp.exp(chunk * lg)
    qdec = jnp.broadcast_to(q_dec[:, :, None], (RET_HEADS, chunk, RET_DK))
    kdec = jnp.broadcast_to(k_dec[:, :, None], (RET_HEADS, chunk, RET_DK))
    sdec = jnp.broadcast_to(s_dec[:, :, None], (RET_HEADS, 1, RET_DV))
    return dmat, qdec, kdec, sdec


def _rope_tables(pos):
    half = RET_DK // 2
    inv_freq = 1.0 / (RET_THETA_BASE ** jnp.linspace(0.0, 1.0, half, dtype=F32))
    ang = pos.astype(F32)[:, None] * inv_freq[None, :]
    return jnp.cos(ang), jnp.sin(ang)


def retention(p, row0, n_streams, length, chunk, pos, s0, per_stream_s0, rows_pref=512):
    n_sub = max(1, min(rows_pref // chunk, length // chunk))
    while (length // chunk) % n_sub:
        n_sub -= 1
    rows = n_sub * chunk
    steps = length // rows
    assert row0 % rows == 0
    base = row0 // rows
    cos, sin = _rope_tables(pos)
    dmat, qdec, kdec, sdec = _ret_tables(chunk)
    qb, vb = RET_HEADS, (2 * RET_HEADS * RET_DK) // RET_DV

    def prow(b, i):
        return base + b * steps + i

    s0_map = (lambda b, h, i: (b, h, 0, 0)) if per_stream_s0 else (lambda b, h, i: (0, h, 0, 0))
    return pl.pallas_call(
        functools.partial(_ret_kernel, chunk=chunk, n_sub=n_sub),
        grid=(n_streams, RET_HEADS, steps),
        in_specs=[
            pl.BlockSpec((rows, RET_DK), lambda b, h, i: (prow(b, i), h)),
            pl.BlockSpec((rows, RET_DK), lambda b, h, i: (prow(b, i), qb + h)),
            pl.BlockSpec((rows, RET_DV), lambda b, h, i: (prow(b, i), vb + h)),
            pl.BlockSpec((rows, RET_DV), lambda b, h, i: (prow(b, i), vb + RET_HEADS + h)),
            pl.BlockSpec((rows, RET_DK // 2), lambda b, h, i: (i, 0)),
            pl.BlockSpec((rows, RET_DK // 2), lambda b, h, i: (i, 0)),
            pl.BlockSpec((None, chunk, chunk), lambda b, h, i: (h, 0, 0)),
            pl.BlockSpec((None, chunk, RET_DK), lambda b, h, i: (h, 0, 0)),
            pl.BlockSpec((None, chunk, RET_DK), lambda b, h, i: (h, 0, 0)),
            pl.BlockSpec((None, 1, RET_DV), lambda b, h, i: (h, 0, 0)),
            pl.BlockSpec((None, None, RET_DK, RET_DV), s0_map),
        ],
        out_specs=[
            pl.BlockSpec((rows, RET_DV), lambda b, h, i: (b * steps + i, h)),
            pl.BlockSpec((None, None, RET_DK, RET_DV), lambda b, h, i: (b, h, 0, 0)),
        ],
        out_shape=[
            jax.ShapeDtypeStruct((n_streams * length, RET_HEADS * RET_DV), BF16),
            jax.ShapeDtypeStruct((n_streams, RET_HEADS, RET_DK, RET_DV), F32),
        ],
        scratch_shapes=[pltpu.VMEM((RET_DK, RET_DV), F32)],
        compiler_params=_params(3),
        name="retention",
    )(p, p, p, p, cos, sin, dmat, qdec, kdec, sdec, s0)


def _router_kernel(h_ref, g_ref, w_ref, b_ref, c_ref):
    xn = _rms(h_ref[...], g_ref[...])
    logits = jnp.dot(xn, w_ref[...], precision=lax.Precision.HIGHEST,
                     preferred_element_type=F32) + b_ref[...]
    lane = lax.broadcasted_iota(jnp.int32, logits.shape, 1).astype(F32)
    neg = jnp.float32(-jnp.inf)
    logits = jnp.where(lane < N_EXP, logits, neg)
    m1 = jnp.max(logits, axis=-1, keepdims=True)
    i1 = jnp.min(jnp.where(logits == m1, lane, float(LANES)), axis=-1, keepdims=True)
    rest = jnp.where(lane == i1, neg, logits)
    m2 = jnp.max(rest, axis=-1, keepdims=True)
    i2 = jnp.min(jnp.where(rest == m2, lane, float(LANES)), axis=-1, keepdims=True)
    e2 = jnp.exp(m2 - m1)
    den = 1.0 + e2
    c_ref[...] = jnp.where(lane == i1, 1.0 / den, 0.0) + jnp.where(lane == i2, e2 / den, 0.0)


def router(h, g, w_r, b_r, tm_pref=1024):
    m, d = h.shape
    tm = _tile(m, tm_pref)
    w_pad = jnp.zeros((d, LANES), F32).at[:, :N_EXP].set(w_r.astype(F32))
    b_pad = jnp.zeros((1, LANES), F32).at[0, :N_EXP].set(b_r.astype(F32))
    return pl.pallas_call(
        _router_kernel,
        grid=(m // tm,),
        in_specs=[pl.BlockSpec((tm, d), lambda i: (i, 0)),
                  pl.BlockSpec((1, d), lambda i: (0, 0)),
                  pl.BlockSpec((d, LANES), lambda i: (0, 0)),
                  pl.BlockSpec((1, LANES), lambda i: (0, 0))],
        out_specs=pl.BlockSpec((tm, LANES), lambda i: (i, 0)),
        out_shape=jax.ShapeDtypeStruct((m, LANES), F32),
        compiler_params=_params(1),
        name="router",
    )(h, g.reshape(1, d), w_pad, b_pad)


def _moe_kernel(x_ref, c_ref, wg_ref, wu_ref, wo_ref, h_ref, *rest, n_norm):
    g_refs = rest[:n_norm]
    ho_ref = rest[n_norm]
    xn_refs = rest[n_norm + 1:n_norm + 1 + n_norm]
    acc = rest[-1]
    e = pl.program_id(1)

    @pl.when(e == 0)
    def _():
        acc[...] = h_ref[...]

    comb = c_ref[...]
    lane = lax.broadcasted_iota(jnp.int32, comb.shape, 1)
    ce = jnp.sum(jnp.where(lane == e, comb, 0.0), axis=-1, keepdims=True)

    @pl.when(jnp.max(ce) > 0.0)
    def _():
        x = x_ref[...]
        gate = _dot(x, wg_ref[...])
        up = _dot(x, wu_ref[...])
        act = (gate * jax.nn.sigmoid(gate) * up).astype(BF16)
        acc[...] += ce * _dot(act, wo_ref[...])

    @pl.when(e == pl.num_programs(1) - 1)
    def _():
        h = acc[...]
        ho_ref[...] = h
        if n_norm:
            y = h * lax.rsqrt(jnp.mean(h * h, axis=-1, keepdims=True) + EPS)
            for g_ref, xn_ref in zip(g_refs, xn_refs):
                xn_ref[...] = (y * g_ref[...]).astype(xn_ref.dtype)


def moe(x, comb, w_in, w_out, h, gains=(), tm_pref=512):
    m, d = x.shape
    n_exp, _, f2 = w_in.shape
    f = f2 // 2
    tm = _tile(m, tm_pref)
    n_norm = len(gains)
    row = pl.BlockSpec((tm, d), lambda i, e: (i, 0))
    gspec = pl.BlockSpec((1, d), lambda i, e: (0, 0))
    outs = pl.pallas_call(
        functools.partial(_moe_kernel, n_norm=n_norm),
        grid=(m // tm, n_exp),
        in_specs=[row,
                  pl.BlockSpec((tm, LANES), lambda i, e: (i, 0)),
                  pl.BlockSpec((None, d, f), lambda i, e: (e, 0, 0)),
                  pl.BlockSpec((None, d, f), lambda i, e: (e, 0, 1)),
                  pl.BlockSpec((None, f, d), lambda i, e: (e, 0, 0)),
                  row] + [gspec] * n_norm,
        out_specs=[row] * (1 + n_norm),
        out_shape=[jax.ShapeDtypeStruct((m, d), F32)]
        + [jax.ShapeDtypeStruct((m, d), BF16)] * n_norm,
        scratch_shapes=[pltpu.VMEM((tm, d), F32)],
        compiler_params=_params(2),
        name="moe",
    )(x, comb, w_in, w_in, w_out, h, *[g.reshape(1, d) for g in gains])
    return outs[0], tuple(outs[1:])


def _sb_kernel(q_ref, kt_ref, v_ref, o_ref, *, tq, bk, p0):
    s = pl.program_id(2) * tq
    rows = SB_GROUP * tq
    qs = jnp.concatenate([q_ref[:, g * SB_DH:(g + 1) * SB_DH] for g in range(SB_GROUP)], axis=0)
    qpos = p0 + s + lax.rem(lax.broadcasted_iota(jnp.int32, (rows, 1), 0), tq)
    kn = lax.broadcasted_iota(jnp.int32, (bk, bk), 0)
    km = lax.broadcasted_iota(jnp.int32, (bk, bk), 1)
    tri = jnp.where(kn >= km, 1.0, 0.0).astype(BF16)
    n_blk = (p0 + s + tq - 1 + bk - 1) // bk
    n_full = (p0 + s) // bk

    def block(j, carry, masked):
        acc, crun = carry
        z = _dot(qs, kt_ref[j])
        sp = jnp.maximum(z, 0.0) + jnp.log2(1.0 + jnp.exp2(-jnp.abs(z)))
        if masked:
            valid = (j * bk + lax.broadcasted_iota(jnp.int32, (1, bk), 1)) < qpos
            sp = jnp.where(valid, sp, 0.0)
        cum = _dot(sp.astype(BF16), tri)
        a = jnp.exp2(z - cum + crun)
        if masked:
            a = jnp.where(valid, a, 0.0)
        acc = acc + _dot(a.astype(BF16), v_ref[j])
        return acc, crun - cum[:, 0:1]

    carry = (jnp.zeros((rows, SB_DH), F32), jnp.zeros((rows, 1), F32))
    carry = lax.fori_loop(0, n_blk - n_full,
                          lambda i, c: block(n_blk - 1 - i, c, True), carry)
    acc, _ = lax.fori_loop(0, n_full,
                           lambda i, c: block(n_full - 1 - i, c, False), carry)
    for g in range(SB_GROUP):
        o_ref[:, g * SB_DH:(g + 1) * SB_DH] = acc[g * tq:(g + 1) * tq].astype(o_ref.dtype)


def sb_attention(q, k, v, p0, tq_pref=256, bk=256):
    b, lq, _ = q.shape
    n = k.shape[1]
    tq = _tile(lq, tq_pref)
    nblk = -(-n // bk)
    pad = nblk * bk - n

    def blocks(t):
        t = jnp.pad(t.astype(BF16), ((0, 0), (0, pad), (0, 0)))
        return t.reshape(b, nblk, bk, SB_KV_HEADS, SB_DH).transpose(0, 3, 1, 2, 4)

    kt = blocks(k).transpose(0, 1, 2, 4, 3)
    vb = blocks(v)
    gw = SB_GROUP * SB_DH
    return pl.pallas_call(
        functools.partial(_sb_kernel, tq=tq, bk=bk, p0=p0),
        grid=(b, SB_KV_HEADS, lq // tq),
        in_specs=[pl.BlockSpec((None, tq, gw), lambda bi, h, t: (bi, t, h)),
                  pl.BlockSpec((None, None, nblk, SB_DH, bk), lambda bi, h, t: (bi, h, 0, 0, 0)),
                  pl.BlockSpec((None, None, nblk, bk, SB_DH), lambda bi, h, t: (bi, h, 0, 0, 0))],
        out_specs=pl.BlockSpec((None, tq, gw), lambda bi, h, t: (bi, t, h)),
        out_shape=jax.ShapeDtypeStruct((b, lq, SB_HEADS * SB_DH), BF16),
        compiler_params=_params(3),
        name="sb_attention",
    )(q, kt, vb)


def kernel(x_prompt, x_sample, state_ret, cache_k, cache_v, meta_tokens, norm_mix, norm_ffn, w_ret_in, w_ret_out, norm_kv, w_kv, k_norm, w_sb_q, q_norm, w_sb_o, w_ffn_in, w_ffn_out, w_router, b_router, w_exp_in, w_exp_out):
    bp, seq, d = x_prompt.shape
    bs, ds, _ = x_sample.shape
    n_a = w_ret_in.shape[0]
    depth = norm_mix.shape[0]
    p0_s = cache_k.shape[1]
    m_main, m_s = bp * seq, bs * ds

    w_ret_in, w_ret_out, w_kv, w_sb_q, w_sb_o, w_ffn_in, w_ffn_out, w_exp_in, w_exp_out = (
        w.astype(BF16) for w in (w_ret_in, w_ret_out, w_kv, w_sb_q, w_sb_o, w_ffn_in, w_ffn_out,
                                 w_exp_in, w_exp_out))

    def channel(l, h, xn, gains):
        if l % 2 == 0:
            act = swiglu_in(xn, w_ffn_in[l // 2])
            return out_proj(act, w_ffn_out[l // 2], h, gains)
        comb = router(h, norm_ffn[l], w_router[l // 2], b_router[l // 2])
        return moe(xn, comb, w_exp_in[l // 2], w_exp_out[l // 2], h, gains)

    def next_gains(l):
        if l + 1 == depth:
            return ()
        if l + 1 == n_a:
            return (norm_kv, norm_mix[l + 1])
        return (norm_mix[l + 1],)

    h_s = jnp.concatenate([x_sample.reshape(m_s, d), meta_tokens.astype(x_prompt.dtype)], axis=0)
    h_m = x_prompt.reshape(m_main, d)
    xn_s = rmsnorm(h_s, norm_mix[0])
    xn_m = rmsnorm(h_m, norm_mix[0])
    pos_meta = jnp.arange(N_META)
    pos_s = p0_s + jnp.arange(ds)
    pos_m = N_META + jnp.arange(seq)
    zero_state = jnp.zeros((1, RET_HEADS, RET_DK, RET_DV), F32)

    ret_p, ret_s = [], []
    for l in range(n_a):
        p_s = matmul(xn_s, w_ret_in[l])
        o_s, st_s = retention(p_s, 0, bs, ds, ds, pos_s, state_ret[l].astype(F32), True)
        o_meta, st_meta = retention(p_s, m_s, 1, N_META, N_META, pos_meta, zero_state, False)
        p_m = matmul(xn_m, w_ret_in[l])
        o_m, st_m = retention(p_m, 0, bp, seq, CHUNK, pos_m, st_meta, False)
        ret_s.append(st_s.astype(state_ret.dtype))
        ret_p.append(st_m.astype(x_prompt.dtype))
        h_s, (xn_s,) = out_proj(jnp.concatenate([o_s, o_meta], axis=0), w_ret_out[l], h_s,
                                (norm_ffn[l],))
        h_m, (xn_m,) = out_proj(o_m, w_ret_out[l], h_m, (norm_ffn[l],))
        h_s, xns_s = channel(l, h_s, xn_s, next_gains(l))
        h_m, xns_m = channel(l, h_m, xn_m, next_gains(l))
        if l + 1 < n_a:
            (xn_s,), (xn_m,) = xns_s, xns_m

    (xkv_s, xq_s), (xkv_m, xq_m) = xns_s, xns_m
    k_sm, v_sm = kv_proj(xkv_s, w_kv, k_norm)
    k_m, v_m = kv_proj(xkv_m, w_kv, k_norm)
    nkv = SB_KV_HEADS * SB_DH

    def with_meta(meta_rows, frames):
        meta_b = jnp.broadcast_to(meta_rows[None], (bp, N_META, nkv))
        return jnp.concatenate([meta_b, frames.reshape(bp, seq, nkv)], axis=1)

    k_p = with_meta(k_sm[m_s:], k_m)
    v_p = with_meta(v_sm[m_s:], v_m)
    k_s = k_sm[:m_s].reshape(bs, ds, nkv)
    v_s = v_sm[:m_s].reshape(bs, ds, nkv)
    k_all_s = jnp.concatenate([cache_k.reshape(bs, p0_s, nkv), k_s.astype(cache_k.dtype)], axis=1)
    v_all_s = jnp.concatenate([cache_v.reshape(bs, p0_s, nkv), v_s.astype(cache_v.dtype)], axis=1)

    h_s, xn_s = h_s[:m_s], xq_s[:m_s]
    xn_m = xq_m
    for l in range(n_a, depth):
        j = l - n_a
        q_s = q_proj(xn_s, w_sb_q[j], q_norm[j]).reshape(bs, ds, d)
        q_m = q_proj(xn_m, w_sb_q[j], q_norm[j]).reshape(bp, seq, d)
        o_s = sb_attention(q_s, k_all_s, v_all_s, p0_s).reshape(m_s, d)
        o_m = sb_attention(q_m, k_p, v_p, N_META).reshape(m_main, d)
        h_s, (xn_s,) = out_proj(o_s, w_sb_o[j], h_s, (norm_ffn[l],))
        h_m, (xn_m,) = out_proj(o_m, w_sb_o[j], h_m, (norm_ffn[l],))
        h_s, xns_s = channel(l, h_s, xn_s, next_gains(l))
        h_m, xns_m = channel(l, h_m, xn_m, next_gains(l))
        if l + 1 < depth:
            (xn_s,), (xn_m,) = xns_s, xns_m

    y_prompt = h_m.reshape(bp, seq, d)
    y_sample = h_s.reshape(bs, ds, d)
    kv4 = lambda t: t.reshape(t.shape[0], t.shape[1], SB_KV_HEADS, SB_DH)
    return (y_prompt, y_sample, jnp.stack(ret_p), kv4(k_p), kv4(v_p), jnp.stack(ret_s),
            kv4(k_s), kv4(v_s))
```

```python
import functools
import math

import jax
import jax.numpy as jnp
from jax import lax
from jax.experimental import pallas as pl
from jax.experimental.pallas import tpu as pltpu

D_MODEL = 1024
N_META = 16
CHUNK = 64
RET_HEADS = 4
RET_DK = 256
RET_DV = 512
RET_IN = RET_HEADS * (2 * RET_DK + 2 * RET_DV)
RET_THETA_BASE = 10000.0
SB_HEADS = 8
SB_KV_HEADS = 4
SB_GROUP = SB_HEADS // SB_KV_HEADS
SB_DH = 128
N_EXP = 8
EPS = 1e-6
LOG2E = 1.4426950408889634

LANES = 128
BF16_ROWS = 16
VMEM_LIMIT = 56 * 1024 * 1024

F32 = jnp.float32
BF16 = jnp.bfloat16


def _params(n_axes):
    return pltpu.CompilerParams(dimension_semantics=("arbitrary",) * n_axes,
                                vmem_limit_bytes=VMEM_LIMIT)


def _tile(n, pref, mult=BF16_ROWS):
    if n <= pref:
        return n
    t = (pref // mult) * mult
    while t >= mult:
        if n % t == 0:
            return t
        t -= mult
    return n


def _dot(a, b):
    return jnp.dot(a, b, preferred_element_type=F32)


def _rms(x, g):
    return (x * lax.rsqrt(jnp.mean(x * x, axis=-1, keepdims=True) + EPS)) * g


def _rmsnorm_kernel(x_ref, g_ref, o_ref):
    o_ref[...] = _rms(x_ref[...], g_ref[...]).astype(o_ref.dtype)


def rmsnorm(x, g, tm_pref=1024):
    m, d = x.shape
    tm = _tile(m, tm_pref)
    return pl.pallas_call(
        _rmsnorm_kernel,
        grid=(m // tm,),
        in_specs=[pl.BlockSpec((tm, d), lambda i: (i, 0)),
                  pl.BlockSpec((1, d), lambda i: (0, 0))],
        out_specs=pl.BlockSpec((tm, d), lambda i: (i, 0)),
        out_shape=jax.ShapeDtypeStruct((m, d), BF16),
        compiler_params=_params(1),
        name="rmsnorm",
    )(x, g.reshape(1, d))


def _mm_kernel(x_ref, w_ref, o_ref):
    o_ref[...] = _dot(x_ref[...], w_ref[...]).astype(o_ref.dtype)


def matmul(x, w, out_dtype=F32, tm_pref=1024, tn_pref=1024):
    m, k = x.shape
    n = w.shape[1]
    tm = _tile(m, tm_pref)
    tn = _tile(n, tn_pref, LANES)
    return pl.pallas_call(
        _mm_kernel,
        grid=(n // tn, m // tm),
        in_specs=[pl.BlockSpec((tm, k), lambda j, i: (i, 0)),
                  pl.BlockSpec((k, tn), lambda j, i: (0, j))],
        out_specs=pl.BlockSpec((tm, tn), lambda j, i: (i, j)),
        out_shape=jax.ShapeDtypeStruct((m, n), out_dtype),
        compiler_params=_params(2),
        name="matmul",
    )(x, w)


def _swiglu_kernel(x_ref, wg_ref, wu_ref, o_ref):
    x = x_ref[...]
    gate = _dot(x, wg_ref[...])
    up = _dot(x, wu_ref[...])
    o_ref[...] = (gate * jax.nn.sigmoid(gate) * up).astype(o_ref.dtype)


def swiglu_in(x, w_in, tm_pref=512, tn_pref=1408):
    m, k = x.shape
    f = w_in.shape[1] // 2
    tm = _tile(m, tm_pref)
    tn = _tile(f, tn_pref, LANES)
    nj = f // tn
    return pl.pallas_call(
        _swiglu_kernel,
        grid=(nj, m // tm),
        in_specs=[pl.BlockSpec((tm, k), lambda j, i: (i, 0)),
                  pl.BlockSpec((k, tn), lambda j, i: (0, j)),
                  pl.BlockSpec((k, tn), lambda j, i: (0, j + nj))],
        out_specs=pl.BlockSpec((tm, tn), lambda j, i: (i, j)),
        out_shape=jax.ShapeDtypeStruct((m, f), BF16),
        compiler_params=_params(2),
        name="swiglu_in",
    )(x, w_in, w_in)


def _out_kernel(x_ref, w_ref, h_ref, *rest, n_norm):
    g_refs = rest[:n_norm]
    ho_ref = rest[n_norm]
    xn_refs = rest[n_norm + 1:]
    h = h_ref[...] + _dot(x_ref[...], w_ref[...])
    ho_ref[...] = h
    if n_norm:
        y = h * lax.rsqrt(jnp.mean(h * h, axis=-1, keepdims=True) + EPS)
        for g_ref, xn_ref in zip(g_refs, xn_refs):
            xn_ref[...] = (y * g_ref[...]).astype(xn_ref.dtype)


def out_proj(x, w, h, gains=(), tm_pref=512):
    m, k = x.shape
    d = w.shape[1]
    tm = _tile(m, tm_pref)
    n_norm = len(gains)
    row = pl.BlockSpec((tm, d), lambda i: (i, 0))
    gspec = pl.BlockSpec((1, d), lambda i: (0, 0))
    outs = pl.pallas_call(
        functools.partial(_out_kernel, n_norm=n_norm),
        grid=(m // tm,),
        in_specs=[pl.BlockSpec((tm, k), lambda i: (i, 0)),
                  pl.BlockSpec((k, d), lambda i: (0, 0)),
                  row] + [gspec] * n_norm,
        out_specs=[row] * (1 + n_norm),
        out_shape=[jax.ShapeDtypeStruct((m, d), F32)]
        + [jax.ShapeDtypeStruct((m, d), BF16)] * n_norm,
        compiler_params=_params(1),
        name="out_proj",
    )(x, w, h, *[g.reshape(1, d) for g in gains])
    return outs[0], tuple(outs[1:])


def _head_norm(y, g, n_heads):
    parts = []
    for hd in range(n_heads):
        yh = y[:, hd * SB_DH:(hd + 1) * SB_DH]
        parts.append(_rms(yh, g))
    return parts


def _qproj_kernel(x_ref, w_ref, g_ref, o_ref):
    y = _dot(x_ref[...], w_ref[...])
    g = g_ref[...]
    for hd, yh in enumerate(_head_norm(y, g, SB_HEADS)):
        o_ref[:, hd * SB_DH:(hd + 1) * SB_DH] = (yh * (SB_DH ** -0.5 * LOG2E)).astype(o_ref.dtype)


def q_proj(x, w, g, tm_pref=512):
    m, k = x.shape
    n = w.shape[1]
    tm = _tile(m, tm_pref)
    return pl.pallas_call(
        _qproj_kernel,
        grid=(m // tm,),
        in_specs=[pl.BlockSpec((tm, k), lambda i: (i, 0)),
                  pl.BlockSpec((k, n), lambda i: (0, 0)),
                  pl.BlockSpec((1, SB_DH), lambda i: (0, 0))],
        out_specs=pl.BlockSpec((tm, n), lambda i: (i, 0)),
        out_shape=jax.ShapeDtypeStruct((m, n), BF16),
        compiler_params=_params(1),
        name="q_proj",
    )(x, w, g.reshape(1, SB_DH))


def _kvproj_kernel(x_ref, w_ref, g_ref, k_ref, v_ref):
    y = _dot(x_ref[...], w_ref[...])
    nk = SB_KV_HEADS * SB_DH
    g = g_ref[...]
    for hd, yh in enumerate(_head_norm(y[:, :nk], g, SB_KV_HEADS)):
        k_ref[:, hd * SB_DH:(hd + 1) * SB_DH] = yh
    v_ref[...] = y[:, nk:]


def kv_proj(x, w, g, tm_pref=512):
    m, k = x.shape
    n = w.shape[1]
    nk = n // 2
    tm = _tile(m, tm_pref)
    return pl.pallas_call(
        _kvproj_kernel,
        grid=(m // tm,),
        in_specs=[pl.BlockSpec((tm, k), lambda i: (i, 0)),
                  pl.BlockSpec((k, n), lambda i: (0, 0)),
                  pl.BlockSpec((1, SB_DH), lambda i: (0, 0))],
        out_specs=[pl.BlockSpec((tm, nk), lambda i: (i, 0))] * 2,
        out_shape=[jax.ShapeDtypeStruct((m, nk), F32)] * 2,
        compiler_params=_params(1),
        name="kv_proj",
    )(x, w, g.reshape(1, SB_DH))


def _rope(x, cos, sin):
    half = RET_DK // 2
    x1, x2 = x[:, :half], x[:, half:]
    return jnp.concatenate([x1 * cos - x2 * sin, x2 * cos + x1 * sin], axis=1)


def _ret_kernel(pq_ref, pk_ref, pv_ref, pg_ref, cos_ref, sin_ref, dmat_ref, qdec_ref, kdec_ref,
                sdec_ref, s0_ref, o_ref, sout_ref, s_scr, *, chunk, n_sub):
    step = pl.program_id(2)

    @pl.when(step == 0)
    def _():
        s_scr[...] = s0_ref[...]

    dmat = dmat_ref[...]
    qdec = qdec_ref[...]
    kdec = kdec_ref[...]
    sdec = sdec_ref[...]
    for c in range(n_sub):
        rows = pl.ds(c * chunk, chunk)
        cos = cos_ref[rows, :]
        sin = sin_ref[rows, :]
        q = _rope(pq_ref[rows, :], cos, sin)
        k = _rope(pk_ref[rows, :], cos, sin) * (RET_DK ** -0.5)
        v = pv_ref[rows, :].astype(BF16)
        s = s_scr[...]
        scores = lax.dot_general(q.astype(BF16), k.astype(BF16), (((1,), (1,)), ((), ())),
                                 preferred_element_type=F32) * dmat
        o = _dot(scores.astype(BF16), v) + _dot((q * qdec).astype(BF16), s.astype(BF16))
        kd_t = (k * kdec).T.astype(BF16)
        s_scr[...] = sdec * s + _dot(kd_t, v)
        o = o * lax.rsqrt(jnp.mean(o * o, axis=-1, keepdims=True) + EPS)
        g = pg_ref[rows, :]
        o_ref[rows, :] = (g * jax.nn.sigmoid(g) * o).astype(o_ref.dtype)

    @pl.when(step == pl.num_programs(2) - 1)
    def _():
        sout_ref[...] = s_scr[...]


def _ret_tables(chunk):
    lg = jnp.log(1.0 - 2.0 ** (-5.0 - jnp.arange(RET_HEADS, dtype=F32)))[:, None]
    i = jnp.arange(chunk, dtype=F32)
    diff = i[:, None] - i[None, :]
    dmat = jnp.where(diff >= 0, jnp.exp(jnp.maximum(diff, 0.0)[None] * lg[:, :, None]), 0.0)
    q_dec = jnp.exp((i + 1.0)[None] * lg)
    k_dec = jnp.exp((chunk - 1.0 - i)[None] * lg)
    s_dec = jnp.exp(chunk * lg)
    qdec = jnp.broadcast_to(q_dec[:, :, None], (RET_HEADS, chunk, RET_DK))
    kdec = jnp.broadcast_to(k_dec[:, :, None], (RET_HEADS, chunk, RET_DK))
    sdec = jnp.broadcast_to(s_dec[:, :, None], (RET_HEADS, 1, RET_DV))
    return dmat, qdec, kdec, sdec


def _rope_tables(pos):
    half = RET_DK // 2
    inv_freq = 1.0 / (RET_THETA_BASE ** jnp.linspace(0.0, 1.0, half, dtype=F32))
    ang = pos.astype(F32)[:, None] * inv_freq[None, :]
    return jnp.cos(ang), jnp.sin(ang)


def retention(p, row0, n_streams, length, chunk, pos, s0, per_stream_s0, rows_pref=512):
    n_sub = max(1, min(rows_pref // chunk, length // chunk))
    while (length // chunk) % n_sub:
        n_sub -= 1
    rows = n_sub * chunk
    steps = length // rows
    assert row0 % rows == 0
    base = row0 // rows
    cos, sin = _rope_tables(pos)
    dmat, qdec, kdec, sdec = _ret_tables(chunk)
    qb, vb = RET_HEADS, (2 * RET_HEADS * RET_DK) // RET_DV

    def prow(b, i):
        return base + b * steps + i

    s0_map = (lambda b, h, i: (b, h, 0, 0)) if per_stream_s0 else (lambda b, h, i: (0, h, 0, 0))
    return pl.pallas_call(
        functools.partial(_ret_kernel, chunk=chunk, n_sub=n_sub),
        grid=(n_streams, RET_HEADS, steps),
        in_specs=[
            pl.BlockSpec((rows, RET_DK), lambda b, h, i: (prow(b, i), h)),
            pl.BlockSpec((rows, RET_DK), lambda b, h, i: (prow(b, i), qb + h)),
            pl.BlockSpec((rows, RET_DV), lambda b, h, i: (prow(b, i), vb + h)),
            pl.BlockSpec((rows, RET_DV), lambda b, h, i: (prow(b, i), vb + RET_HEADS + h)),
            pl.BlockSpec((rows, RET_DK // 2), lambda b, h, i: (i, 0)),
            pl.BlockSpec((rows, RET_DK // 2), lambda b, h, i: (i, 0)),
            pl.BlockSpec((None, chunk, chunk), lambda b, h, i: (h, 0, 0)),
            pl.BlockSpec((None, chunk, RET_DK), lambda b, h, i: (h, 0, 0)),
            pl.BlockSpec((None, chunk, RET_DK), lambda b, h, i: (h, 0, 0)),
            pl.BlockSpec((None, 1, RET_DV), lambda b, h, i: (h, 0, 0)),
            pl.BlockSpec((None, None, RET_DK, RET_DV), s0_map),
        ],
        out_specs=[
            pl.BlockSpec((rows, RET_DV), lambda b, h, i: (b * steps + i, h)),
            pl.BlockSpec((None, None, RET_DK, RET_DV), lambda b, h, i: (b, h, 0, 0)),
        ],
        out_shape=[
            jax.ShapeDtypeStruct((n_streams * length, RET_HEADS * RET_DV), BF16),
            jax.ShapeDtypeStruct((n_streams, RET_HEADS, RET_DK, RET_DV), F32),
        ],
        scratch_shapes=[pltpu.VMEM((RET_DK, RET_DV), F32)],
        compiler_params=_params(3),
        name="retention",
    )(p, p, p, p, cos, sin, dmat, qdec, kdec, sdec, s0)


def _router_kernel(h_ref, g_ref, w_ref, b_ref, c_ref):
    xn = _rms(h_ref[...], g_ref[...])
    logits = jnp.dot(xn, w_ref[...], precision=lax.Precision.HIGHEST,
                     preferred_element_type=F32) + b_ref[...]
    lane = lax.broadcasted_iota(jnp.int32, logits.shape, 1).astype(F32)
    neg = jnp.float32(-jnp.inf)
    logits = jnp.where(lane < N_EXP, logits, neg)
    m1 = jnp.max(logits, axis=-1, keepdims=True)
    i1 = jnp.min(jnp.where(logits == m1, lane, float(LANES)), axis=-1, keepdims=True)
    rest = jnp.where(lane == i1, neg, logits)
    m2 = jnp.max(rest, axis=-1, keepdims=True)
    i2 = jnp.min(jnp.where(rest == m2, lane, float(LANES)), axis=-1, keepdims=True)
    e2 = jnp.exp(m2 - m1)
    den = 1.0 + e2
    c_ref[...] = jnp.where(lane == i1, 1.0 / den, 0.0) + jnp.where(lane == i2, e2 / den, 0.0)


def router(h, g, w_r, b_r, tm_pref=1024):
    m, d = h.shape
    tm = _tile(m, tm_pref)
    w_pad = jnp.zeros((d, LANES), F32).at[:, :N_EXP].set(w_r.astype(F32))
    b_pad = jnp.zeros((1, LANES), F32).at[0, :N_EXP].set(b_r.astype(F32))
    return pl.pallas_call(
        _router_kernel,
        grid=(m // tm,),
        in_specs=[pl.BlockSpec((tm, d), lambda i: (i, 0)),
                  pl.BlockSpec((1, d), lambda i: (0, 0)),
                  pl.BlockSpec((d, LANES), lambda i: (0, 0)),
                  pl.BlockSpec((1, LANES), lambda i: (0, 0))],
        out_specs=pl.BlockSpec((tm, LANES), lambda i: (i, 0)),
        out_shape=jax.ShapeDtypeStruct((m, LANES), F32),
        compiler_params=_params(1),
        name="router",
    )(h, g.reshape(1, d), w_pad, b_pad)


def _moe_kernel(x_ref, c_ref, wg_ref, wu_ref, wo_ref, h_ref, *rest, n_norm):
    g_refs = rest[:n_norm]
    ho_ref = rest[n_norm]
    xn_refs = rest[n_norm + 1:n_norm + 1 + n_norm]
    acc = rest[-1]
    e = pl.program_id(1)

    @pl.when(e == 0)
    def _():
        acc[...] = h_ref[...]

    comb = c_ref[...]
    lane = lax.broadcasted_iota(jnp.int32, comb.shape, 1)
    ce = jnp.sum(jnp.where(lane == e, comb, 0.0), axis=-1, keepdims=True)

    @pl.when(jnp.max(ce) > 0.0)
    def _():
        x = x_ref[...]
        gate = _dot(x, wg_ref[...])
        up = _dot(x, wu_ref[...])
        act = (gate * jax.nn.sigmoid(gate) * up).astype(BF16)
        acc[...] += ce * _dot(act, wo_ref[...])

    @pl.when(e == pl.num_programs(1) - 1)
    def _():
        h = acc[...]
        ho_ref[...] = h
        if n_norm:
            y = h * lax.rsqrt(jnp.mean(h * h, axis=-1, keepdims=True) + EPS)
            for g_ref, xn_ref in zip(g_refs, xn_refs):
                xn_ref[...] = (y * g_ref[...]).astype(xn_ref.dtype)


def moe(x, comb, w_in, w_out, h, gains=(), tm_pref=512):
    m, d = x.shape
    n_exp, _, f2 = w_in.shape
    f = f2 // 2
    tm = _tile(m, tm_pref)
    n_norm = len(gains)
    row = pl.BlockSpec((tm, d), lambda i, e: (i, 0))
    gspec = pl.BlockSpec((1, d), lambda i, e: (0, 0))
    outs = pl.pallas_call(
        functools.partial(_moe_kernel, n_norm=n_norm),
        grid=(m // tm, n_exp),
        in_specs=[row,
                  pl.BlockSpec((tm, LANES), lambda i, e: (i, 0)),
                  pl.BlockSpec((None, d, f), lambda i, e: (e, 0, 0)),
                  pl.BlockSpec((None, d, f), lambda i, e: (e, 0, 1)),
                  pl.BlockSpec((None, f, d), lambda i, e: (e, 0, 0)),
                  row] + [gspec] * n_norm,
        out_specs=[row] * (1 + n_norm),
        out_shape=[jax.ShapeDtypeStruct((m, d), F32)]
        + [jax.ShapeDtypeStruct((m, d), BF16)] * n_norm,
        scratch_shapes=[pltpu.VMEM((tm, d), F32)],
        compiler_params=_params(2),
        name="moe",
    )(x, comb, w_in, w_in, w_out, h, *[g.reshape(1, d) for g in gains])
    return outs[0], tuple(outs[1:])


def _sb_kernel(q_ref, kt_ref, v_ref, o_ref, *, tq, bk, p0):
    s = pl.program_id(2) * tq
    rows = SB_GROUP * tq
    qs = jnp.concatenate([q_ref[:, g * SB_DH:(g + 1) * SB_DH] for g in range(SB_GROUP)], axis=0)
    qpos = p0 + s + lax.rem(lax.broadcasted_iota(jnp.int32, (rows, 1), 0), tq)
    kn = lax.broadcasted_iota(jnp.int32, (bk, bk), 0)
    km = lax.broadcasted_iota(jnp.int32, (bk, bk), 1)
    tri = jnp.where(kn >= km, 1.0, 0.0).astype(BF16)
    n_blk = (p0 + s + tq - 1 + bk - 1) // bk
    n_full = (p0 + s) // bk

    def block(j, carry, masked):
        acc, crun = carry
        z = _dot(qs, kt_ref[j])
        sp = _softplus2(z)
        if masked:
            valid = (j * bk + lax.broadcasted_iota(jnp.int32, (1, bk), 1)) < qpos
            sp = jnp.where(valid, sp, 0.0)
        cum = _dot(sp.astype(BF16), tri)
        a = jnp.exp2(z - cum + crun)
        if masked:
            a = jnp.where(valid, a, 0.0)
        acc = acc + _dot(a.astype(BF16), v_ref[j])
        return acc, crun - cum[:, 0:1]

    carry = (jnp.zeros((rows, SB_DH), F32), jnp.zeros((rows, 1), F32))
    carry = lax.fori_loop(0, n_blk - n_full,
                          lambda i, c: block(n_blk - 1 - i, c, True), carry)
    acc, _ = lax.fori_loop(0, n_full,
                           lambda i, c: block(n_full - 1 - i, c, False), carry)
    for g in range(SB_GROUP):
        o_ref[:, g * SB_DH:(g + 1) * SB_DH] = acc[g * tq:(g + 1) * tq].astype(o_ref.dtype)


def _softplus2(z):
    return jnp.maximum(jnp.log2(1.0 + jnp.exp2(jnp.minimum(z, 126.0))), z)


def _sb_pipe_kernel(q_ref, k_ref, vt_ref, o_ref, acc_ref, z_buf, d_buf, *, bk, p0):
    s = pl.program_id(2) * bk
    rows = SB_GROUP * bk
    qs = jnp.concatenate([q_ref[:, g * SB_DH:(g + 1) * SB_DH] for g in range(SB_GROUP)], axis=0)
    qt = qs.T
    qpos = p0 + s + lax.rem(lax.broadcasted_iota(jnp.int32, (1, rows), 1), bk)
    kn = lax.broadcasted_iota(jnp.int32, (bk, bk), 0)
    km = lax.broadcasted_iota(jnp.int32, (bk, bk), 1)
    trit = jnp.where(km >= kn, 1.0, 0.0).astype(BF16)
    n_full = (p0 + s) // bk
    n_masked = (p0 % bk + 2 * bk - 2) // bk

    krow = lax.broadcasted_iota(jnp.int32, (bk, 1), 0)
    js = [n_full + n_masked - 1 - m for m in range(n_masked)]
    zs = [_dot(k_ref[j], qt) for j in js]
    valids = [(j * bk + krow) < qpos for j in js]
    crun = jnp.zeros((1, rows), F32)
    ds = []
    for z, valid in zip(zs, valids):
        cum = _dot(trit, jnp.where(valid, _softplus2(z), 0.0).astype(BF16))
        ds.append(cum - crun)
        crun = crun - cum[0:1, :]
    acc = jnp.zeros((SB_DH, rows), F32)
    for j, z, valid, d in zip(js, zs, valids, ds):
        acc = acc + _dot(vt_ref[j], jnp.where(valid, jnp.exp2(z - d), 0.0).astype(BF16))
    acc_ref[...] = acc

    zero_blk = k_ref.shape[0] - 1

    def blk(u):
        return jnp.where(u < n_full, n_full - 1 - u, zero_blk)

    def stage_z(u, slot):
        z_buf[slot] = _dot(k_ref[blk(u)], qt)

    def stage_d(slot, crun):
        cum = _dot(trit, _softplus2(z_buf[slot]).astype(BF16))
        d_buf[slot] = cum - crun
        return crun - cum[0:1, :]

    def stage_acc(u, slot):
        a = jnp.exp2(z_buf[slot] - d_buf[slot])
        acc_ref[...] += _dot(vt_ref[blk(u)], a.astype(BF16))

    stage_z(0, 0)
    stage_z(1, 1)
    crun = stage_d(0, crun)

    def steady(i, crun):
        for r in range(3):
            stage_z(3 * i + r + 2, (r + 2) % 3)
            stage_acc(3 * i + r, r)
            crun = stage_d((r + 1) % 3, crun)
        return crun

    lax.fori_loop(0, (n_full + 2) // 3, steady, crun)
    acc = acc_ref[...].T
    for g in range(SB_GROUP):
        o_ref[:, g * SB_DH:(g + 1) * SB_DH] = acc[g * bk:(g + 1) * bk].astype(o_ref.dtype)


def _key_blocks(t, nblk, bk):
    b, n, _ = t.shape
    t = jnp.pad(t.astype(BF16), ((0, 0), (0, nblk * bk - n), (0, 0)))
    return t.reshape(b, nblk, bk, SB_KV_HEADS, SB_DH).transpose(0, 3, 1, 2, 4)


def sb_attention_pipelined(q, k, v, p0, bk=256):
    b, lq, _ = q.shape
    nblk = -(-k.shape[1] // bk) + 1
    kb = _key_blocks(k, nblk, bk)
    vt = _key_blocks(v, nblk, bk).transpose(0, 1, 2, 4, 3)
    gw = SB_GROUP * SB_DH
    rows = SB_GROUP * bk
    return pl.pallas_call(
        functools.partial(_sb_pipe_kernel, bk=bk, p0=p0),
        grid=(b, SB_KV_HEADS, lq // bk),
        in_specs=[pl.BlockSpec((None, bk, gw), lambda bi, h, t: (bi, t, h)),
                  pl.BlockSpec((None, None, nblk, bk, SB_DH), lambda bi, h, t: (bi, h, 0, 0, 0)),
                  pl.BlockSpec((None, None, nblk, SB_DH, bk), lambda bi, h, t: (bi, h, 0, 0, 0))],
        out_specs=pl.BlockSpec((None, bk, gw), lambda bi, h, t: (bi, t, h)),
        out_shape=jax.ShapeDtypeStruct((b, lq, SB_HEADS * SB_DH), BF16),
        scratch_shapes=[pltpu.VMEM((SB_DH, rows), F32),
                        pltpu.VMEM((3, bk, rows), F32),
                        pltpu.VMEM((3, bk, rows), F32)],
        compiler_params=_params(3),
        name="sb_attention_pipelined",
    )(q, kb, vt)


def sb_attention(q, k, v, p0, tq_pref=256, bk=256):
    b, lq, _ = q.shape
    if lq % bk == 0:
        return sb_attention_pipelined(q, k, v, p0, bk)
    tq = _tile(lq, tq_pref)
    nblk = -(-k.shape[1] // bk)
    vb = _key_blocks(v, nblk, bk)
    kt = _key_blocks(k, nblk, bk).transpose(0, 1, 2, 4, 3)
    gw = SB_GROUP * SB_DH
    return pl.pallas_call(
        functools.partial(_sb_kernel, tq=tq, bk=bk, p0=p0),
        grid=(b, SB_KV_HEADS, lq // tq),
        in_specs=[pl.BlockSpec((None, tq, gw), lambda bi, h, t: (bi, t, h)),
                  pl.BlockSpec((None, None, nblk, SB_DH, bk), lambda bi, h, t: (bi, h, 0, 0, 0)),
                  pl.BlockSpec((None, None, nblk, bk, SB_DH), lambda bi, h, t: (bi, h, 0, 0, 0))],
        out_specs=pl.BlockSpec((None, tq, gw), lambda bi, h, t: (bi, t, h)),
        out_shape=jax.ShapeDtypeStruct((b, lq, SB_HEADS * SB_DH), BF16),
        compiler_params=_params(3),
        name="sb_attention",
    )(q, kt, vb)


def kernel(x_prompt, x_sample, state_ret, cache_k, cache_v, meta_tokens, norm_mix, norm_ffn, w_ret_in, w_ret_out, norm_kv, w_kv, k_norm, w_sb_q, q_norm, w_sb_o, w_ffn_in, w_ffn_out, w_router, b_router, w_exp_in, w_exp_out):
    bp, seq, d = x_prompt.shape
    bs, ds, _ = x_sample.shape
    n_a = w_ret_in.shape[0]
    depth = norm_mix.shape[0]
    p0_s = cache_k.shape[1]
    m_main, m_s = bp * seq, bs * ds

    w_ret_in, w_ret_out, w_kv, w_sb_q, w_sb_o, w_ffn_in, w_ffn_out, w_exp_in, w_exp_out = (
        w.astype(BF16) for w in (w_ret_in, w_ret_out, w_kv, w_sb_q, w_sb_o, w_ffn_in, w_ffn_out,
                                 w_exp_in, w_exp_out))

    def channel(l, h, xn, gains):
        if l % 2 == 0:
            act = swiglu_in(xn, w_ffn_in[l // 2])
            return out_proj(act, w_ffn_out[l // 2], h, gains)
        comb = router(h, norm_ffn[l], w_router[l // 2], b_router[l // 2])
        return moe(xn, comb, w_exp_in[l // 2], w_exp_out[l // 2], h, gains)

    def next_gains(l):
        if l + 1 == depth:
            return ()
        if l + 1 == n_a:
            return (norm_kv, norm_mix[l + 1])
        return (norm_mix[l + 1],)

    h_s = jnp.concatenate([x_sample.reshape(m_s, d), meta_tokens.astype(x_prompt.dtype)], axis=0)
    h_m = x_prompt.reshape(m_main, d)
    xn_s = rmsnorm(h_s, norm_mix[0])
    xn_m = rmsnorm(h_m, norm_mix[0])
    pos_meta = jnp.arange(N_META)
    pos_s = p0_s + jnp.arange(ds)
    pos_m = N_META + jnp.arange(seq)
    zero_state = jnp.zeros((1, RET_HEADS, RET_DK, RET_DV), F32)

    ret_p, ret_s = [], []
    for l in range(n_a):
        p_s = matmul(xn_s, w_ret_in[l])
        o_s, st_s = retention(p_s, 0, bs, ds, ds, pos_s, state_ret[l].astype(F32), True)
        o_meta, st_meta = retention(p_s, m_s, 1, N_META, N_META, pos_meta, zero_state, False)
        p_m = matmul(xn_m, w_ret_in[l])
        o_m, st_m = retention(p_m, 0, bp, seq, CHUNK, pos_m, st_meta, False)
        ret_s.append(st_s.astype(state_ret.dtype))
        ret_p.append(st_m.astype(x_prompt.dtype))
        h_s, (xn_s,) = out_proj(jnp.concatenate([o_s, o_meta], axis=0), w_ret_out[l], h_s,
                                (norm_ffn[l],))
        h_m, (xn_m,) = out_proj(o_m, w_ret_out[l], h_m, (norm_ffn[l],))
        h_s, xns_s = channel(l, h_s, xn_s, next_gains(l))
        h_m, xns_m = channel(l, h_m, xn_m, next_gains(l))
        if l + 1 < n_a:
            (xn_s,), (xn_m,) = xns_s, xns_m

    (xkv_s, xq_s), (xkv_m, xq_m) = xns_s, xns_m
    k_sm, v_sm = kv_proj(xkv_s, w_kv, k_norm)
    k_m, v_m = kv_proj(xkv_m, w_kv, k_norm)
    nkv = SB_KV_HEADS * SB_DH

    def with_meta(meta_rows, frames):
        meta_b = jnp.broadcast_to(meta_rows[None], (bp, N_META, nkv))
        return jnp.concatenate([meta_b, frames.reshape(bp, seq, nkv)], axis=1)

    k_p = with_meta(k_sm[m_s:], k_m)
    v_p = with_meta(v_sm[m_s:], v_m)
    k_s = k_sm[:m_s].reshape(bs, ds, nkv)
    v_s = v_sm[:m_s].reshape(bs, ds, nkv)
    k_all_s = jnp.concatenate([cache_k.reshape(bs, p0_s, nkv), k_s.astype(cache_k.dtype)], axis=1)
    v_all_s = jnp.concatenate([cache_v.reshape(bs, p0_s, nkv), v_s.astype(cache_v.dtype)], axis=1)

    h_s, xn_s = h_s[:m_s], xq_s[:m_s]
    xn_m = xq_m
    for l in range(n_a, depth):
        j = l - n_a
        q_s = q_proj(xn_s, w_sb_q[j], q_norm[j]).reshape(bs, ds, d)
        q_m = q_proj(xn_m, w_sb_q[j], q_norm[j]).reshape(bp, seq, d)
        o_s = sb_attention(q_s, k_all_s, v_all_s, p0_s).reshape(m_s, d)
        o_m = sb_attention(q_m, k_p, v_p, N_META).reshape(m_main, d)
        h_s, (xn_s,) = out_proj(o_s, w_sb_o[j], h_s, (norm_ffn[l],))
        h_m, (xn_m,) = out_proj(o_m, w_sb_o[j], h_m, (norm_ffn[l],))
        h_s, xns_s = channel(l, h_s, xn_s, next_gains(l))
        h_m, xns_m = channel(l, h_m, xn_m, next_gains(l))
        if l + 1 < depth:
            (xn_s,), (xn_m,) = xns_s, xns_m

    y_prompt = h_m.reshape(bp, seq, d)
    y_sample = h_s.reshape(bs, ds, d)
    kv4 = lambda t: t.reshape(t.shape[0], t.shape[1], SB_KV_HEADS, SB_DH)
    return (y_prompt, y_sample, jnp.stack(ret_p), kv4(k_p), kv4(v_p), jnp.stack(ret_s),
            kv4(k_s), kv4(v_s))
```

```python
import functools

import jax
import jax.numpy as jnp
from jax import lax
from jax.experimental import pallas as pl
from jax.experimental.pallas import tpu as pltpu

D_MODEL = 1024
N_META = 16
RET_BLOCK = 256
SB_BLOCK = 256
RET_HEADS = 4
RET_DK = 256
RET_DV = 512
RET_IN = RET_HEADS * (2 * RET_DK + 2 * RET_DV)
RET_THETA_BASE = 10000.0
SB_HEADS = 8
SB_KV_HEADS = 4
SB_GROUP = SB_HEADS // SB_KV_HEADS
SB_DH = 128
N_EXP = 8
EPS = 1e-6
LOG2E = 1.4426950408889634

LANES = 128
BF16_ROWS = 16
VMEM_LIMIT = 56 * 1024 * 1024

F32 = jnp.float32
BF16 = jnp.bfloat16


def _params(n_axes):
    return pltpu.CompilerParams(dimension_semantics=("arbitrary",) * n_axes,
                                vmem_limit_bytes=VMEM_LIMIT)


def _tile(n, pref, mult=BF16_ROWS):
    if n <= pref:
        return n
    t = (pref // mult) * mult
    while t >= mult:
        if n % t == 0:
            return t
        t -= mult
    return n


def _dot(a, b):
    return jnp.dot(a, b, preferred_element_type=F32)


def _rms(x, g):
    return (x * lax.rsqrt(jnp.mean(x * x, axis=-1, keepdims=True) + EPS)) * g


def _rmsnorm_kernel(x_ref, g_ref, o_ref):
    o_ref[...] = _rms(x_ref[...], g_ref[...]).astype(o_ref.dtype)


def rmsnorm(x, g, tm_pref=1024):
    m, d = x.shape
    tm = _tile(m, tm_pref)
    return pl.pallas_call(
        _rmsnorm_kernel,
        grid=(m // tm,),
        in_specs=[pl.BlockSpec((tm, d), lambda i: (i, 0)),
                  pl.BlockSpec((1, d), lambda i: (0, 0))],
        out_specs=pl.BlockSpec((tm, d), lambda i: (i, 0)),
        out_shape=jax.ShapeDtypeStruct((m, d), BF16),
        compiler_params=_params(1),
        name="rmsnorm",
    )(x, g.reshape(1, d))


def _mm_kernel(x_ref, w_ref, o_ref):
    o_ref[...] = _dot(x_ref[...], w_ref[...]).astype(o_ref.dtype)


def matmul(x, w, out_dtype=F32, tm_pref=1024, tn_pref=1024):
    m, k = x.shape
    n = w.shape[1]
    tm = _tile(m, tm_pref)
    tn = _tile(n, tn_pref, LANES)
    return pl.pallas_call(
        _mm_kernel,
        grid=(n // tn, m // tm),
        in_specs=[pl.BlockSpec((tm, k), lambda j, i: (i, 0)),
                  pl.BlockSpec((k, tn), lambda j, i: (0, j))],
        out_specs=pl.BlockSpec((tm, tn), lambda j, i: (i, j)),
        out_shape=jax.ShapeDtypeStruct((m, n), out_dtype),
        compiler_params=_params(2),
        name="matmul",
    )(x, w)


def _swiglu_kernel(x_ref, wg_ref, wu_ref, o_ref):
    x = x_ref[...]
    gate = _dot(x, wg_ref[...])
    up = _dot(x, wu_ref[...])
    o_ref[...] = (gate * jax.nn.sigmoid(gate) * up).astype(o_ref.dtype)


def swiglu_in(x, w_in, tm_pref=512, tn_pref=1408):
    m, k = x.shape
    f = w_in.shape[1] // 2
    tm = _tile(m, tm_pref)
    tn = _tile(f, tn_pref, LANES)
    nj = f // tn
    return pl.pallas_call(
        _swiglu_kernel,
        grid=(nj, m // tm),
        in_specs=[pl.BlockSpec((tm, k), lambda j, i: (i, 0)),
                  pl.BlockSpec((k, tn), lambda j, i: (0, j)),
                  pl.BlockSpec((k, tn), lambda j, i: (0, j + nj))],
        out_specs=pl.BlockSpec((tm, tn), lambda j, i: (i, j)),
        out_shape=jax.ShapeDtypeStruct((m, f), BF16),
        compiler_params=_params(2),
        name="swiglu_in",
    )(x, w_in, w_in)


def _out_kernel(x_ref, w_ref, h_ref, *rest, n_norm):
    g_refs = rest[:n_norm]
    ho_ref = rest[n_norm]
    xn_refs = rest[n_norm + 1:]
    h = h_ref[...] + _dot(x_ref[...], w_ref[...])
    ho_ref[...] = h
    if n_norm:
        y = h * lax.rsqrt(jnp.mean(h * h, axis=-1, keepdims=True) + EPS)
        for g_ref, xn_ref in zip(g_refs, xn_refs):
            xn_ref[...] = (y * g_ref[...]).astype(xn_ref.dtype)


def out_proj(x, w, h, gains=(), tm_pref=512):
    m, k = x.shape
    d = w.shape[1]
    tm = _tile(m, tm_pref)
    n_norm = len(gains)
    row = pl.BlockSpec((tm, d), lambda i: (i, 0))
    gspec = pl.BlockSpec((1, d), lambda i: (0, 0))
    outs = pl.pallas_call(
        functools.partial(_out_kernel, n_norm=n_norm),
        grid=(m // tm,),
        in_specs=[pl.BlockSpec((tm, k), lambda i: (i, 0)),
                  pl.BlockSpec((k, d), lambda i: (0, 0)),
                  row] + [gspec] * n_norm,
        out_specs=[row] * (1 + n_norm),
        out_shape=[jax.ShapeDtypeStruct((m, d), F32)]
        + [jax.ShapeDtypeStruct((m, d), BF16)] * n_norm,
        compiler_params=_params(1),
        name="out_proj",
    )(x, w, h, *[g.reshape(1, d) for g in gains])
    return outs[0], tuple(outs[1:])


def _head_norm(y, g, n_heads):
    parts = []
    for hd in range(n_heads):
        yh = y[:, hd * SB_DH:(hd + 1) * SB_DH]
        parts.append(_rms(yh, g))
    return parts


def _qproj_kernel(x_ref, w_ref, g_ref, o_ref):
    y = _dot(x_ref[...], w_ref[...])
    g = g_ref[...]
    for hd, yh in enumerate(_head_norm(y, g, SB_HEADS)):
        o_ref[:, hd * SB_DH:(hd + 1) * SB_DH] = (yh * (SB_DH ** -0.5 * LOG2E)).astype(o_ref.dtype)


def q_proj(x, w, g, tm_pref=512):
    m, k = x.shape
    n = w.shape[1]
    tm = _tile(m, tm_pref)
    return pl.pallas_call(
        _qproj_kernel,
        grid=(m // tm,),
        in_specs=[pl.BlockSpec((tm, k), lambda i: (i, 0)),
                  pl.BlockSpec((k, n), lambda i: (0, 0)),
                  pl.BlockSpec((1, SB_DH), lambda i: (0, 0))],
        out_specs=pl.BlockSpec((tm, n), lambda i: (i, 0)),
        out_shape=jax.ShapeDtypeStruct((m, n), BF16),
        compiler_params=_params(1),
        name="q_proj",
    )(x, w, g.reshape(1, SB_DH))


def _kvproj_kernel(x_ref, w_ref, g_ref, k_ref, v_ref):
    y = _dot(x_ref[...], w_ref[...])
    nk = SB_KV_HEADS * SB_DH
    g = g_ref[...]
    for hd, yh in enumerate(_head_norm(y[:, :nk], g, SB_KV_HEADS)):
        k_ref[:, hd * SB_DH:(hd + 1) * SB_DH] = yh
    v_ref[...] = y[:, nk:]


def kv_proj(x, w, g, tm_pref=512):
    m, k = x.shape
    n = w.shape[1]
    nk = n // 2
    tm = _tile(m, tm_pref)
    return pl.pallas_call(
        _kvproj_kernel,
        grid=(m // tm,),
        in_specs=[pl.BlockSpec((tm, k), lambda i: (i, 0)),
                  pl.BlockSpec((k, n), lambda i: (0, 0)),
                  pl.BlockSpec((1, SB_DH), lambda i: (0, 0))],
        out_specs=[pl.BlockSpec((tm, nk), lambda i: (i, 0))] * 2,
        out_shape=[jax.ShapeDtypeStruct((m, nk), F32)] * 2,
        compiler_params=_params(1),
        name="kv_proj",
    )(x, w, g.reshape(1, SB_DH))


def _kvproj_blocks_kernel(x_ref, w_ref, g_ref, k_ref, v_ref, kb_ref, vt_ref, *, bk):
    y = _dot(x_ref[...], w_ref[...])
    nk = SB_KV_HEADS * SB_DH
    n_sub = x_ref.shape[0] // bk
    g = g_ref[...]
    for hd, yh in enumerate(_head_norm(y[:, :nk], g, SB_KV_HEADS)):
        k_ref[:, hd * SB_DH:(hd + 1) * SB_DH] = yh
        for r in range(n_sub):
            kb_ref[hd, r] = yh[r * bk:(r + 1) * bk].astype(BF16)
    v_ref[...] = y[:, nk:]
    for hd in range(SB_KV_HEADS):
        vh = y[:, nk + hd * SB_DH:nk + (hd + 1) * SB_DH]
        for r in range(n_sub):
            vt_ref[hd, r] = vh[r * bk:(r + 1) * bk].T.astype(BF16)


def kv_proj_blocks(x, w, g, batch, bk, n_extra, tm_pref=512):
    m, k = x.shape
    n = w.shape[1]
    nk = n // 2
    length = m // batch
    tm = tm_pref if length % tm_pref == 0 else bk
    assert length % tm == 0 and tm % bk == 0
    steps, n_sub = length // tm, tm // bk
    nblk = length // bk + n_extra
    return pl.pallas_call(
        functools.partial(_kvproj_blocks_kernel, bk=bk),
        grid=(batch, steps),
        in_specs=[pl.BlockSpec((tm, k), lambda b, i: (b * steps + i, 0)),
                  pl.BlockSpec((k, n), lambda b, i: (0, 0)),
                  pl.BlockSpec((1, SB_DH), lambda b, i: (0, 0))],
        out_specs=[pl.BlockSpec((tm, nk), lambda b, i: (b * steps + i, 0))] * 2
        + [pl.BlockSpec((None, SB_KV_HEADS, n_sub, bk, SB_DH), lambda b, i: (b, 0, i, 0, 0)),
           pl.BlockSpec((None, SB_KV_HEADS, n_sub, SB_DH, bk), lambda b, i: (b, 0, i, 0, 0))],
        out_shape=[jax.ShapeDtypeStruct((m, nk), F32)] * 2
        + [jax.ShapeDtypeStruct((batch, SB_KV_HEADS, nblk, bk, SB_DH), BF16),
           jax.ShapeDtypeStruct((batch, SB_KV_HEADS, nblk, SB_DH, bk), BF16)],
        compiler_params=_params(2),
        name="kv_proj_blocks",
    )(x, w, g.reshape(1, SB_DH))


def _rope(x, cos, sin):
    half = RET_DK // 2
    x1, x2 = x[:, :half], x[:, half:]
    return jnp.concatenate([x1 * cos - x2 * sin, x2 * cos + x1 * sin], axis=1)


def _ret_kernel(pq_ref, pk_ref, pv_ref, pg_ref, cos_ref, sin_ref, dmat_ref, qdec_ref, kdec_ref,
                sdec_ref, s0_ref, o_ref, sout_ref, s_scr, *, chunk, n_sub):
    step = pl.program_id(2)

    @pl.when(step == 0)
    def _():
        s_scr[...] = s0_ref[...]

    dmat = dmat_ref[...]
    qdec = qdec_ref[...]
    kdec = kdec_ref[...]
    sdec = sdec_ref[...]
    for c in range(n_sub):
        rows = pl.ds(c * chunk, chunk)
        cos = cos_ref[rows, :]
        sin = sin_ref[rows, :]
        q = _rope(pq_ref[rows, :], cos, sin)
        k = _rope(pk_ref[rows, :], cos, sin) * (RET_DK ** -0.5)
        v = pv_ref[rows, :].astype(BF16)
        s = s_scr[...]
        scores = lax.dot_general(q.astype(BF16), k.astype(BF16), (((1,), (1,)), ((), ())),
                                 preferred_element_type=F32) * dmat
        o = _dot(scores.astype(BF16), v) + _dot((q * qdec).astype(BF16), s.astype(BF16))
        kd_t = (k * kdec).T.astype(BF16)
        s_scr[...] = sdec * s + _dot(kd_t, v)
        o = o * lax.rsqrt(jnp.mean(o * o, axis=-1, keepdims=True) + EPS)
        g = pg_ref[rows, :]
        o_ref[rows, :] = (g * jax.nn.sigmoid(g) * o).astype(o_ref.dtype)

    @pl.when(step == pl.num_programs(2) - 1)
    def _():
        sout_ref[...] = s_scr[...]


def _ret_tables(chunk):
    lg = jnp.log(1.0 - 2.0 ** (-5.0 - jnp.arange(RET_HEADS, dtype=F32)))[:, None]
    i = jnp.arange(chunk, dtype=F32)
    diff = i[:, None] - i[None, :]
    dmat = jnp.where(diff >= 0, jnp.exp(jnp.maximum(diff, 0.0)[None] * lg[:, :, None]), 0.0)
    q_dec = jnp.exp((i + 1.0)[None] * lg)
    k_dec = jnp.exp((chunk - 1.0 - i)[None] * lg)
    s_dec = jnp.exp(chunk * lg)
    qdec = jnp.broadcast_to(q_dec[:, :, None], (RET_HEADS, chunk, RET_DK))
    kdec = jnp.broadcast_to(k_dec[:, :, None], (RET_HEADS, chunk, RET_DK))
    sdec = jnp.broadcast_to(s_dec[:, :, None], (RET_HEADS, 1, RET_DV))
    return dmat, qdec, kdec, sdec


def _rope_tables(pos):
    half = RET_DK // 2
    inv_freq = 1.0 / (RET_THETA_BASE ** jnp.linspace(0.0, 1.0, half, dtype=F32))
    ang = pos.astype(F32)[:, None] * inv_freq[None, :]
    return jnp.cos(ang), jnp.sin(ang)


def retention(p, row0, n_streams, length, chunk, pos, s0, per_stream_s0, rows_pref=512):
    n_sub = max(1, min(rows_pref // chunk, length // chunk))
    while (length // chunk) % n_sub:
        n_sub -= 1
    rows = n_sub * chunk
    steps = length // rows
    assert row0 % rows == 0
    base = row0 // rows
    cos, sin = _rope_tables(pos)
    dmat, qdec, kdec, sdec = _ret_tables(chunk)
    qb, vb = RET_HEADS, (2 * RET_HEADS * RET_DK) // RET_DV

    def prow(b, i):
        return base + b * steps + i

    s0_map = (lambda b, h, i: (b, h, 0, 0)) if per_stream_s0 else (lambda b, h, i: (0, h, 0, 0))
    return pl.pallas_call(
        functools.partial(_ret_kernel, chunk=chunk, n_sub=n_sub),
        grid=(n_streams, RET_HEADS, steps),
        in_specs=[
            pl.BlockSpec((rows, RET_DK), lambda b, h, i: (prow(b, i), h)),
            pl.BlockSpec((rows, RET_DK), lambda b, h, i: (prow(b, i), qb + h)),
            pl.BlockSpec((rows, RET_DV), lambda b, h, i: (prow(b, i), vb + h)),
            pl.BlockSpec((rows, RET_DV), lambda b, h, i: (prow(b, i), vb + RET_HEADS + h)),
            pl.BlockSpec((rows, RET_DK // 2), lambda b, h, i: (i, 0)),
            pl.BlockSpec((rows, RET_DK // 2), lambda b, h, i: (i, 0)),
            pl.BlockSpec((None, chunk, chunk), lambda b, h, i: (h, 0, 0)),
            pl.BlockSpec((None, chunk, RET_DK), lambda b, h, i: (h, 0, 0)),
            pl.BlockSpec((None, chunk, RET_DK), lambda b, h, i: (h, 0, 0)),
            pl.BlockSpec((None, 1, RET_DV), lambda b, h, i: (h, 0, 0)),
            pl.BlockSpec((None, None, RET_DK, RET_DV), s0_map),
        ],
        out_specs=[
            pl.BlockSpec((rows, RET_DV), lambda b, h, i: (b * steps + i, h)),
            pl.BlockSpec((None, None, RET_DK, RET_DV), lambda b, h, i: (b, h, 0, 0)),
        ],
        out_shape=[
            jax.ShapeDtypeStruct((n_streams * length, RET_HEADS * RET_DV), BF16),
            jax.ShapeDtypeStruct((n_streams, RET_HEADS, RET_DK, RET_DV), F32),
        ],
        scratch_shapes=[pltpu.VMEM((RET_DK, RET_DV), F32)],
        compiler_params=_params(3),
        name="retention",
    )(p, p, p, p, cos, sin, dmat, qdec, kdec, sdec, s0)


def _router_kernel(h_ref, g_ref, w_ref, b_ref, c_ref):
    xn = _rms(h_ref[...], g_ref[...])
    logits = jnp.dot(xn, w_ref[...], precision=lax.Precision.HIGHEST,
                     preferred_element_type=F32) + b_ref[...]
    lane = lax.broadcasted_iota(jnp.int32, logits.shape, 1).astype(F32)
    neg = jnp.float32(-jnp.inf)
    logits = jnp.where(lane < N_EXP, logits, neg)
    m1 = jnp.max(logits, axis=-1, keepdims=True)
    i1 = jnp.min(jnp.where(logits == m1, lane, float(LANES)), axis=-1, keepdims=True)
    rest = jnp.where(lane == i1, neg, logits)
    m2 = jnp.max(rest, axis=-1, keepdims=True)
    i2 = jnp.min(jnp.where(rest == m2, lane, float(LANES)), axis=-1, keepdims=True)
    e2 = jnp.exp(m2 - m1)
    den = 1.0 + e2
    c_ref[...] = jnp.where(lane == i1, 1.0 / den, 0.0) + jnp.where(lane == i2, e2 / den, 0.0)


def router(h, g, w_r, b_r, tm_pref=1024):
    m, d = h.shape
    tm = _tile(m, tm_pref)
    w_pad = jnp.zeros((d, LANES), F32).at[:, :N_EXP].set(w_r.astype(F32))
    b_pad = jnp.zeros((1, LANES), F32).at[0, :N_EXP].set(b_r.astype(F32))
    return pl.pallas_call(
        _router_kernel,
        grid=(m // tm,),
        in_specs=[pl.BlockSpec((tm, d), lambda i: (i, 0)),
                  pl.BlockSpec((1, d), lambda i: (0, 0)),
                  pl.BlockSpec((d, LANES), lambda i: (0, 0)),
                  pl.BlockSpec((1, LANES), lambda i: (0, 0))],
        out_specs=pl.BlockSpec((tm, LANES), lambda i: (i, 0)),
        out_shape=jax.ShapeDtypeStruct((m, LANES), F32),
        compiler_params=_params(1),
        name="router",
    )(h, g.reshape(1, d), w_pad, b_pad)


def _moe_kernel(x_ref, c_ref, wg_ref, wu_ref, wo_ref, h_ref, *rest, n_norm):
    g_refs = rest[:n_norm]
    ho_ref = rest[n_norm]
    xn_refs = rest[n_norm + 1:n_norm + 1 + n_norm]
    acc = rest[-1]
    e = pl.program_id(1)

    @pl.when(e == 0)
    def _():
        acc[...] = h_ref[...]

    comb = c_ref[...]
    lane = lax.broadcasted_iota(jnp.int32, comb.shape, 1)
    ce = jnp.sum(jnp.where(lane == e, comb, 0.0), axis=-1, keepdims=True)

    @pl.when(jnp.max(ce) > 0.0)
    def _():
        x = x_ref[...]
        gate = _dot(x, wg_ref[...])
        up = _dot(x, wu_ref[...])
        act = (gate * jax.nn.sigmoid(gate) * up).astype(BF16)
        acc[...] += ce * _dot(act, wo_ref[...])

    @pl.when(e == pl.num_programs(1) - 1)
    def _():
        h = acc[...]
        ho_ref[...] = h
        if n_norm:
            y = h * lax.rsqrt(jnp.mean(h * h, axis=-1, keepdims=True) + EPS)
            for g_ref, xn_ref in zip(g_refs, xn_refs):
                xn_ref[...] = (y * g_ref[...]).astype(xn_ref.dtype)


def moe(x, comb, w_in, w_out, h, gains=(), tm_pref=512):
    m, d = x.shape
    n_exp, _, f2 = w_in.shape
    f = f2 // 2
    tm = _tile(m, tm_pref)
    n_norm = len(gains)
    row = pl.BlockSpec((tm, d), lambda i, e: (i, 0))
    gspec = pl.BlockSpec((1, d), lambda i, e: (0, 0))
    outs = pl.pallas_call(
        functools.partial(_moe_kernel, n_norm=n_norm),
        grid=(m // tm, n_exp),
        in_specs=[row,
                  pl.BlockSpec((tm, LANES), lambda i, e: (i, 0)),
                  pl.BlockSpec((None, d, f), lambda i, e: (e, 0, 0)),
                  pl.BlockSpec((None, d, f), lambda i, e: (e, 0, 1)),
                  pl.BlockSpec((None, f, d), lambda i, e: (e, 0, 0)),
                  row] + [gspec] * n_norm,
        out_specs=[row] * (1 + n_norm),
        out_shape=[jax.ShapeDtypeStruct((m, d), F32)]
        + [jax.ShapeDtypeStruct((m, d), BF16)] * n_norm,
        scratch_shapes=[pltpu.VMEM((tm, d), F32)],
        compiler_params=_params(2),
        name="moe",
    )(x, comb, w_in, w_in, w_out, h, *[g.reshape(1, d) for g in gains])
    return outs[0], tuple(outs[1:])


def _softplus2(z):
    return jnp.maximum(jnp.log2(1.0 + jnp.exp2(jnp.minimum(z, 126.0))), z)


def _suffix_ones(n, keys_on_sublanes):
    r = lax.broadcasted_iota(jnp.int32, (n, n), 0)
    c = lax.broadcasted_iota(jnp.int32, (n, n), 1)
    return jnp.where((c >= r) if keys_on_sublanes else (r >= c), 1.0, 0.0).astype(BF16)


def _sb_prompt_kernel(q_ref, k_ref, vt_ref, o_ref, acc_ref, z_buf, d_buf, *, tq, bk, n_frame_blocks):
    t = pl.program_id(2)
    rows = SB_GROUP * tq
    qs = jnp.concatenate([q_ref[:, g * SB_DH:(g + 1) * SB_DH] for g in range(SB_GROUP)], axis=0)
    qt = qs.T
    qfrm = t * tq + lax.rem(lax.broadcasted_iota(jnp.int32, (1, rows), 1), tq)
    trit = _suffix_ones(bk, True)
    per_tile = tq // bk
    n_before = t * per_tile

    krow = lax.broadcasted_iota(jnp.int32, (bk, 1), 0)
    js = [n_before + per_tile - 1 - m for m in range(per_tile)]
    zs = [_dot(k_ref[j], qt) for j in js]
    valids = [(j * bk + krow) < qfrm for j in js]
    crun = jnp.zeros((1, rows), F32)
    ds = []
    for z, valid in zip(zs, valids):
        cum = _dot(trit, jnp.where(valid, _softplus2(z), 0.0).astype(BF16))
        ds.append(cum - crun)
        crun = crun - cum[0:1, :]
    acc = jnp.zeros((SB_DH, rows), F32)
    for j, z, valid, d in zip(js, zs, valids, ds):
        acc = acc + _dot(vt_ref[j], jnp.where(valid, jnp.exp2(z - d), 0.0).astype(BF16))
    acc_ref[...] = acc

    def blk(u):
        return jnp.where(u < n_before, n_before - 1 - u,
                         jnp.minimum(n_frame_blocks + u - n_before, n_frame_blocks + 1))

    def stage_z(u, slot):
        z_buf[slot] = _dot(k_ref[blk(u)], qt)

    def stage_d(slot, crun):
        cum = _dot(trit, _softplus2(z_buf[slot]).astype(BF16))
        d_buf[slot] = cum - crun
        return crun - cum[0:1, :]

    def stage_acc(u, slot):
        a = jnp.exp2(z_buf[slot] - d_buf[slot])
        acc_ref[...] += _dot(vt_ref[blk(u)], a.astype(BF16))

    stage_z(0, 0)
    stage_z(1, 1)
    crun = stage_d(0, crun)

    def steady(i, crun):
        for r in range(3):
            stage_z(3 * i + r + 2, (r + 2) % 3)
            stage_acc(3 * i + r, r)
            crun = stage_d((r + 1) % 3, crun)
        return crun

    lax.fori_loop(0, (n_before + 1 + 2) // 3, steady, crun)
    acc = acc_ref[...].T
    for g in range(SB_GROUP):
        o_ref[:, g * SB_DH:(g + 1) * SB_DH] = acc[g * tq:(g + 1) * tq].astype(o_ref.dtype)


def sb_attention_prompt(q, kb, vt, bk, tq_pref=512):
    b, lq, _ = q.shape
    nblk = kb.shape[2]
    tq = tq_pref if lq % tq_pref == 0 else bk
    assert lq % tq == 0 and tq % bk == 0 and nblk == lq // bk + 2
    gw = SB_GROUP * SB_DH
    rows = SB_GROUP * tq
    return pl.pallas_call(
        functools.partial(_sb_prompt_kernel, tq=tq, bk=bk, n_frame_blocks=lq // bk),
        grid=(b, SB_KV_HEADS, lq // tq),
        in_specs=[pl.BlockSpec((None, tq, gw), lambda bi, h, t: (bi, t, h)),
                  pl.BlockSpec((None, None, nblk, bk, SB_DH), lambda bi, h, t: (bi, h, 0, 0, 0)),
                  pl.BlockSpec((None, None, nblk, SB_DH, bk), lambda bi, h, t: (bi, h, 0, 0, 0))],
        out_specs=pl.BlockSpec((None, tq, gw), lambda bi, h, t: (bi, t, h)),
        out_shape=jax.ShapeDtypeStruct((b, lq, SB_HEADS * SB_DH), BF16),
        scratch_shapes=[pltpu.VMEM((SB_DH, rows), F32),
                        pltpu.VMEM((3, bk, rows), F32),
                        pltpu.VMEM((3, bk, rows), F32)],
        compiler_params=_params(3),
        name="sb_attention_prompt",
    )(q, kb, vt)


def _sb_sample_kernel(q_ref, kc_ref, vc_ref, kn_ref, vn_ref, o_ref, *, bk):
    ds = q_ref.shape[0]
    n_cache = kc_ref.shape[0]
    rows = SB_GROUP * ds
    qs = jnp.concatenate([q_ref[:, g * SB_DH:(g + 1) * SB_DH] for g in range(SB_GROUP)], axis=0)
    qrow = lax.rem(lax.broadcasted_iota(jnp.int32, (rows, 1), 0), ds)
    new_valid = lax.broadcasted_iota(jnp.int32, (1, ds), 1) < qrow

    spans = [(kn_ref, vn_ref, 0, ds, new_valid)]
    whole = n_cache // bk
    if n_cache % bk:
        spans.append((kc_ref, vc_ref, whole * bk, n_cache % bk, None))
    spans += [(kc_ref, vc_ref, j * bk, bk, None) for j in reversed(range(whole))]

    zs = [lax.dot_general(qs, kr[st:st + sz, :].astype(BF16), (((1,), (1,)), ((), ())),
                          preferred_element_type=F32) for kr, _, st, sz, _ in spans]
    tris = {sz: _suffix_ones(sz, False) for sz in {sp[3] for sp in spans}}
    crun = jnp.zeros((rows, 1), F32)
    dds = []
    for z, (_, _, _, sz, valid) in zip(zs, spans):
        sp = _softplus2(z)
        if valid is not None:
            sp = jnp.where(valid, sp, 0.0)
        cum = _dot(sp.astype(BF16), tris[sz])
        dds.append(cum - crun)
        crun = crun - cum[:, 0:1]
    acc = jnp.zeros((rows, SB_DH), F32)
    for z, d, (_, vr, st, sz, valid) in zip(zs, dds, spans):
        a = jnp.exp2(z - d)
        if valid is not None:
            a = jnp.where(valid, a, 0.0)
        acc = acc + _dot(a.astype(BF16), vr[st:st + sz, :].astype(BF16))
    for g in range(SB_GROUP):
        o_ref[:, g * SB_DH:(g + 1) * SB_DH] = acc[g * ds:(g + 1) * ds].astype(o_ref.dtype)


def sb_attention_sample(q, cache_k, cache_v, k_new, v_new, bk=256):
    b, ds, _ = q.shape
    n_cache = cache_k.shape[1]
    gw = SB_GROUP * SB_DH
    cache = pl.BlockSpec((None, n_cache, SB_DH), lambda bi, h: (bi, 0, h))
    new = pl.BlockSpec((None, ds, SB_DH), lambda bi, h: (bi, 0, h))
    qo = pl.BlockSpec((None, ds, gw), lambda bi, h: (bi, 0, h))
    return pl.pallas_call(
        functools.partial(_sb_sample_kernel, bk=bk),
        grid=(b, SB_KV_HEADS),
        in_specs=[qo, cache, cache, new, new],
        out_specs=qo,
        out_shape=jax.ShapeDtypeStruct((b, ds, SB_HEADS * SB_DH), BF16),
        compiler_params=_params(2),
        name="sb_attention_sample",
    )(q, cache_k, cache_v, k_new, v_new)


def kernel(x_prompt, x_sample, state_ret, cache_k, cache_v, meta_tokens, norm_mix, norm_ffn, w_ret_in, w_ret_out, norm_kv, w_kv, k_norm, w_sb_q, q_norm, w_sb_o, w_ffn_in, w_ffn_out, w_router, b_router, w_exp_in, w_exp_out):
    bp, seq, d = x_prompt.shape
    bs, ds, _ = x_sample.shape
    n_a = w_ret_in.shape[0]
    depth = norm_mix.shape[0]
    p0_s = cache_k.shape[1]
    m_main, m_s = bp * seq, bs * ds

    w_ret_in, w_ret_out, w_kv, w_sb_q, w_sb_o, w_ffn_in, w_ffn_out, w_exp_in, w_exp_out = (
        w.astype(BF16) for w in (w_ret_in, w_ret_out, w_kv, w_sb_q, w_sb_o, w_ffn_in, w_ffn_out,
                                 w_exp_in, w_exp_out))

    def channel(l, h, xn, gains):
        if l % 2 == 0:
            act = swiglu_in(xn, w_ffn_in[l // 2])
            return out_proj(act, w_ffn_out[l // 2], h, gains)
        comb = router(h, norm_ffn[l], w_router[l // 2], b_router[l // 2])
        return moe(xn, comb, w_exp_in[l // 2], w_exp_out[l // 2], h, gains)

    def next_gains(l):
        if l + 1 == depth:
            return ()
        if l + 1 == n_a:
            return (norm_kv, norm_mix[l + 1])
        return (norm_mix[l + 1],)

    h_s = jnp.concatenate([x_sample.reshape(m_s, d), meta_tokens.astype(x_prompt.dtype)], axis=0)
    h_m = x_prompt.reshape(m_main, d)
    xn_s = rmsnorm(h_s, norm_mix[0])
    xn_m = rmsnorm(h_m, norm_mix[0])
    pos_meta = jnp.arange(N_META)
    pos_s = p0_s + jnp.arange(ds)
    pos_m = N_META + jnp.arange(seq)
    zero_state = jnp.zeros((1, RET_HEADS, RET_DK, RET_DV), F32)

    ret_p, ret_s = [], []
    for l in range(n_a):
        p_s = matmul(xn_s, w_ret_in[l])
        o_s, st_s = retention(p_s, 0, bs, ds, ds, pos_s, state_ret[l].astype(F32), True)
        o_meta, st_meta = retention(p_s, m_s, 1, N_META, N_META, pos_meta, zero_state, False)
        p_m = matmul(xn_m, w_ret_in[l])
        o_m, st_m = retention(p_m, 0, bp, seq, RET_BLOCK, pos_m, st_meta, False)
        ret_s.append(st_s.astype(state_ret.dtype))
        ret_p.append(st_m.astype(x_prompt.dtype))
        h_s, (xn_s,) = out_proj(jnp.concatenate([o_s, o_meta], axis=0), w_ret_out[l], h_s,
                                (norm_ffn[l],))
        h_m, (xn_m,) = out_proj(o_m, w_ret_out[l], h_m, (norm_ffn[l],))
        h_s, xns_s = channel(l, h_s, xn_s, next_gains(l))
        h_m, xns_m = channel(l, h_m, xn_m, next_gains(l))
        if l + 1 < n_a:
            (xn_s,), (xn_m,) = xns_s, xns_m

    (xkv_s, xq_s), (xkv_m, xq_m) = xns_s, xns_m
    k_sm, v_sm = kv_proj(xkv_s, w_kv, k_norm)
    k_m, v_m, kb, vt = kv_proj_blocks(xkv_m, w_kv, k_norm, bp, SB_BLOCK, 2)
    nkv = SB_KV_HEADS * SB_DH
    k_meta, v_meta = k_sm[m_s:], v_sm[m_s:]

    def extra_blocks(meta_rows):
        blk = jnp.pad(meta_rows.astype(BF16), ((SB_BLOCK - N_META, 0), (0, 0)))
        blk = blk.reshape(SB_BLOCK, SB_KV_HEADS, SB_DH).transpose(1, 0, 2)
        both = jnp.stack([blk, jnp.zeros_like(blk)], axis=1)
        return jnp.broadcast_to(both[None], (bp,) + both.shape)

    n_fb = seq // SB_BLOCK
    kb = lax.dynamic_update_slice(kb, extra_blocks(k_meta), (0, 0, n_fb, 0, 0))
    vt = lax.dynamic_update_slice(vt, extra_blocks(v_meta).transpose(0, 1, 2, 4, 3), (0, 0, n_fb, 0, 0))

    def with_meta(meta_rows, frames):
        meta_b = jnp.broadcast_to(meta_rows[None], (bp, N_META, nkv))
        return jnp.concatenate([meta_b, frames.reshape(bp, seq, nkv)], axis=1)

    k_p = with_meta(k_meta, k_m)
    v_p = with_meta(v_meta, v_m)
    k_s = k_sm[:m_s].reshape(bs, ds, nkv)
    v_s = v_sm[:m_s].reshape(bs, ds, nkv)
    cache_k2 = cache_k.reshape(bs, p0_s, nkv)
    cache_v2 = cache_v.reshape(bs, p0_s, nkv)

    h_s, xn_s = h_s[:m_s], xq_s[:m_s]
    xn_m = xq_m
    for l in range(n_a, depth):
        j = l - n_a
        q_s = q_proj(xn_s, w_sb_q[j], q_norm[j]).reshape(bs, ds, d)
        q_m = q_proj(xn_m, w_sb_q[j], q_norm[j]).reshape(bp, seq, d)
        o_s = sb_attention_sample(q_s, cache_k2, cache_v2, k_s.astype(cache_k.dtype),
                                  v_s.astype(cache_v.dtype), SB_BLOCK).reshape(m_s, d)
        o_m = sb_attention_prompt(q_m, kb, vt, SB_BLOCK).reshape(m_main, d)
        h_s, (xn_s,) = out_proj(o_s, w_sb_o[j], h_s, (norm_ffn[l],))
        h_m, (xn_m,) = out_proj(o_m, w_sb_o[j], h_m, (norm_ffn[l],))
        h_s, xns_s = channel(l, h_s, xn_s, next_gains(l))
        h_m, xns_m = channel(l, h_m, xn_m, next_gains(l))
        if l + 1 < depth:
            (xn_s,), (xn_m,) = xns_s, xns_m

    y_prompt = h_m.reshape(bp, seq, d)
    y_sample = h_s.reshape(bs, ds, d)
    kv4 = lambda t: t.reshape(t.shape[0], t.shape[1], SB_KV_HEADS, SB_DH)
    return (y_prompt, y_sample, jnp.stack(ret_p), kv4(k_p), kv4(v_p), jnp.stack(ret_s),
            kv4(k_s), kv4(v_s))
```

```python
import functools

import jax
import jax.numpy as jnp
from jax import lax
from jax.experimental import pallas as pl
from jax.experimental.pallas import tpu as pltpu

D_MODEL = 1024
N_META = 16
RET_BLOCK = 256
SB_BLOCK = 256
MOE_TILE = 512
MOE_CHUNK = 160
RET_HEADS = 4
RET_DK = 256
RET_DV = 512
RET_IN = RET_HEADS * (2 * RET_DK + 2 * RET_DV)
RET_THETA_BASE = 10000.0
SB_HEADS = 8
SB_KV_HEADS = 4
SB_GROUP = SB_HEADS // SB_KV_HEADS
SB_DH = 128
N_EXP = 8
EPS = 1e-6
LOG2E = 1.4426950408889634

LANES = 128
BF16_ROWS = 16
VMEM_LIMIT = 56 * 1024 * 1024

F32 = jnp.float32
BF16 = jnp.bfloat16


def _params(n_axes):
    return pltpu.CompilerParams(dimension_semantics=("arbitrary",) * n_axes,
                                vmem_limit_bytes=VMEM_LIMIT)


def _tile(n, pref, mult=BF16_ROWS):
    if n <= pref:
        return n
    t = (pref // mult) * mult
    while t >= mult:
        if n % t == 0:
            return t
        t -= mult
    return n


def _dot(a, b):
    return jnp.dot(a, b, preferred_element_type=F32)


def _rms(x, g):
    return (x * lax.rsqrt(jnp.mean(x * x, axis=-1, keepdims=True) + EPS)) * g


def _rmsnorm_kernel(x_ref, g_ref, o_ref):
    o_ref[...] = _rms(x_ref[...], g_ref[...]).astype(o_ref.dtype)


def rmsnorm(x, g, tm_pref=1024):
    m, d = x.shape
    tm = _tile(m, tm_pref)
    return pl.pallas_call(
        _rmsnorm_kernel,
        grid=(m // tm,),
        in_specs=[pl.BlockSpec((tm, d), lambda i: (i, 0)),
                  pl.BlockSpec((1, d), lambda i: (0, 0))],
        out_specs=pl.BlockSpec((tm, d), lambda i: (i, 0)),
        out_shape=jax.ShapeDtypeStruct((m, d), BF16),
        compiler_params=_params(1),
        name="rmsnorm",
    )(x, g.reshape(1, d))


def _mm_kernel(x_ref, w_ref, o_ref):
    o_ref[...] = _dot(x_ref[...], w_ref[...]).astype(o_ref.dtype)


def matmul(x, w, out_dtype=F32, tm_pref=1024, tn_pref=1024):
    m, k = x.shape
    n = w.shape[1]
    tm = _tile(m, tm_pref)
    tn = _tile(n, tn_pref, LANES)
    return pl.pallas_call(
        _mm_kernel,
        grid=(n // tn, m // tm),
        in_specs=[pl.BlockSpec((tm, k), lambda j, i: (i, 0)),
                  pl.BlockSpec((k, tn), lambda j, i: (0, j))],
        out_specs=pl.BlockSpec((tm, tn), lambda j, i: (i, j)),
        out_shape=jax.ShapeDtypeStruct((m, n), out_dtype),
        compiler_params=_params(2),
        name="matmul",
    )(x, w)


def _swiglu_kernel(x_ref, wg_ref, wu_ref, o_ref):
    x = x_ref[...]
    gate = _dot(x, wg_ref[...])
    up = _dot(x, wu_ref[...])
    o_ref[...] = (gate * jax.nn.sigmoid(gate) * up).astype(o_ref.dtype)


def swiglu_in(x, w_in, tm_pref=512, tn_pref=1408):
    m, k = x.shape
    f = w_in.shape[1] // 2
    tm = _tile(m, tm_pref)
    tn = _tile(f, tn_pref, LANES)
    nj = f // tn
    return pl.pallas_call(
        _swiglu_kernel,
        grid=(nj, m // tm),
        in_specs=[pl.BlockSpec((tm, k), lambda j, i: (i, 0)),
                  pl.BlockSpec((k, tn), lambda j, i: (0, j)),
                  pl.BlockSpec((k, tn), lambda j, i: (0, j + nj))],
        out_specs=pl.BlockSpec((tm, tn), lambda j, i: (i, j)),
        out_shape=jax.ShapeDtypeStruct((m, f), BF16),
        compiler_params=_params(2),
        name="swiglu_in",
    )(x, w_in, w_in)


def _out_kernel(x_ref, w_ref, h_ref, *rest, n_norm):
    g_refs = rest[:n_norm]
    ho_ref = rest[n_norm]
    xn_refs = rest[n_norm + 1:]
    h = h_ref[...] + _dot(x_ref[...], w_ref[...])
    ho_ref[...] = h
    if n_norm:
        y = h * lax.rsqrt(jnp.mean(h * h, axis=-1, keepdims=True) + EPS)
        for g_ref, xn_ref in zip(g_refs, xn_refs):
            xn_ref[...] = (y * g_ref[...]).astype(xn_ref.dtype)


def out_proj(x, w, h, gains=(), tm_pref=512):
    m, k = x.shape
    d = w.shape[1]
    tm = _tile(m, tm_pref)
    n_norm = len(gains)
    row = pl.BlockSpec((tm, d), lambda i: (i, 0))
    gspec = pl.BlockSpec((1, d), lambda i: (0, 0))
    outs = pl.pallas_call(
        functools.partial(_out_kernel, n_norm=n_norm),
        grid=(m // tm,),
        in_specs=[pl.BlockSpec((tm, k), lambda i: (i, 0)),
                  pl.BlockSpec((k, d), lambda i: (0, 0)),
                  row] + [gspec] * n_norm,
        out_specs=[row] * (1 + n_norm),
        out_shape=[jax.ShapeDtypeStruct((m, d), F32)]
        + [jax.ShapeDtypeStruct((m, d), BF16)] * n_norm,
        compiler_params=_params(1),
        name="out_proj",
    )(x, w, h, *[g.reshape(1, d) for g in gains])
    return outs[0], tuple(outs[1:])


def _head_norm(y, g, n_heads):
    parts = []
    for hd in range(n_heads):
        yh = y[:, hd * SB_DH:(hd + 1) * SB_DH]
        parts.append(_rms(yh, g))
    return parts


def _qproj_kernel(x_ref, w_ref, g_ref, o_ref):
    y = _dot(x_ref[...], w_ref[...])
    g = g_ref[...]
    for hd, yh in enumerate(_head_norm(y, g, SB_HEADS)):
        o_ref[:, hd * SB_DH:(hd + 1) * SB_DH] = (yh * (SB_DH ** -0.5 * LOG2E)).astype(o_ref.dtype)


def q_proj(x, w, g, tm_pref=512):
    m, k = x.shape
    n = w.shape[1]
    tm = _tile(m, tm_pref)
    return pl.pallas_call(
        _qproj_kernel,
        grid=(m // tm,),
        in_specs=[pl.BlockSpec((tm, k), lambda i: (i, 0)),
                  pl.BlockSpec((k, n), lambda i: (0, 0)),
                  pl.BlockSpec((1, SB_DH), lambda i: (0, 0))],
        out_specs=pl.BlockSpec((tm, n), lambda i: (i, 0)),
        out_shape=jax.ShapeDtypeStruct((m, n), BF16),
        compiler_params=_params(1),
        name="q_proj",
    )(x, w, g.reshape(1, SB_DH))


def _kvproj_kernel(x_ref, w_ref, g_ref, k_ref, v_ref):
    y = _dot(x_ref[...], w_ref[...])
    nk = SB_KV_HEADS * SB_DH
    g = g_ref[...]
    for hd, yh in enumerate(_head_norm(y[:, :nk], g, SB_KV_HEADS)):
        k_ref[:, hd * SB_DH:(hd + 1) * SB_DH] = yh
    v_ref[...] = y[:, nk:]


def kv_proj(x, w, g, tm_pref=512):
    m, k = x.shape
    n = w.shape[1]
    nk = n // 2
    tm = _tile(m, tm_pref)
    return pl.pallas_call(
        _kvproj_kernel,
        grid=(m // tm,),
        in_specs=[pl.BlockSpec((tm, k), lambda i: (i, 0)),
                  pl.BlockSpec((k, n), lambda i: (0, 0)),
                  pl.BlockSpec((1, SB_DH), lambda i: (0, 0))],
        out_specs=[pl.BlockSpec((tm, nk), lambda i: (i, 0))] * 2,
        out_shape=[jax.ShapeDtypeStruct((m, nk), F32)] * 2,
        compiler_params=_params(1),
        name="kv_proj",
    )(x, w, g.reshape(1, SB_DH))


def _kvproj_blocks_kernel(x_ref, w_ref, g_ref, k_ref, v_ref, kb_ref, vt_ref, *, bk):
    y = _dot(x_ref[...], w_ref[...])
    nk = SB_KV_HEADS * SB_DH
    n_sub = x_ref.shape[0] // bk
    g = g_ref[...]
    for hd, yh in enumerate(_head_norm(y[:, :nk], g, SB_KV_HEADS)):
        k_ref[:, hd * SB_DH:(hd + 1) * SB_DH] = yh
        for r in range(n_sub):
            kb_ref[hd, r] = yh[r * bk:(r + 1) * bk].astype(BF16)
    v_ref[...] = y[:, nk:]
    for hd in range(SB_KV_HEADS):
        vh = y[:, nk + hd * SB_DH:nk + (hd + 1) * SB_DH]
        for r in range(n_sub):
            vt_ref[hd, r] = vh[r * bk:(r + 1) * bk].T.astype(BF16)


def kv_proj_blocks(x, w, g, batch, bk, n_extra, tm_pref=512):
    m, k = x.shape
    n = w.shape[1]
    nk = n // 2
    length = m // batch
    tm = tm_pref if length % tm_pref == 0 else bk
    assert length % tm == 0 and tm % bk == 0
    steps, n_sub = length // tm, tm // bk
    nblk = length // bk + n_extra
    return pl.pallas_call(
        functools.partial(_kvproj_blocks_kernel, bk=bk),
        grid=(batch, steps),
        in_specs=[pl.BlockSpec((tm, k), lambda b, i: (b * steps + i, 0)),
                  pl.BlockSpec((k, n), lambda b, i: (0, 0)),
                  pl.BlockSpec((1, SB_DH), lambda b, i: (0, 0))],
        out_specs=[pl.BlockSpec((tm, nk), lambda b, i: (b * steps + i, 0))] * 2
        + [pl.BlockSpec((None, SB_KV_HEADS, n_sub, bk, SB_DH), lambda b, i: (b, 0, i, 0, 0)),
           pl.BlockSpec((None, SB_KV_HEADS, n_sub, SB_DH, bk), lambda b, i: (b, 0, i, 0, 0))],
        out_shape=[jax.ShapeDtypeStruct((m, nk), F32)] * 2
        + [jax.ShapeDtypeStruct((batch, SB_KV_HEADS, nblk, bk, SB_DH), BF16),
           jax.ShapeDtypeStruct((batch, SB_KV_HEADS, nblk, SB_DH, bk), BF16)],
        compiler_params=_params(2),
        name="kv_proj_blocks",
    )(x, w, g.reshape(1, SB_DH))


def _rope(x, cos, sin):
    half = RET_DK // 2
    x1, x2 = x[:, :half], x[:, half:]
    return jnp.concatenate([x1 * cos - x2 * sin, x2 * cos + x1 * sin], axis=1)


def _ret_kernel(pq_ref, pk_ref, pv_ref, pg_ref, cos_ref, sin_ref, dmat_ref, qdec_ref, kdec_ref,
                sdec_ref, s0_ref, o_ref, sout_ref, s_scr, *, chunk, n_sub):
    step = pl.program_id(2)

    @pl.when(step == 0)
    def _():
        s_scr[...] = s0_ref[...]

    dmat = dmat_ref[...]
    qdec = qdec_ref[...]
    kdec = kdec_ref[...]
    sdec = sdec_ref[...]
    for c in range(n_sub):
        rows = pl.ds(c * chunk, chunk)
        cos = cos_ref[rows, :]
        sin = sin_ref[rows, :]
        q = _rope(pq_ref[rows, :], cos, sin)
        k = _rope(pk_ref[rows, :], cos, sin) * (RET_DK ** -0.5)
        v = pv_ref[rows, :].astype(BF16)
        s = s_scr[...]
        scores = lax.dot_general(q.astype(BF16), k.astype(BF16), (((1,), (1,)), ((), ())),
                                 preferred_element_type=F32) * dmat
        o = _dot(scores.astype(BF16), v) + _dot((q * qdec).astype(BF16), s.astype(BF16))
        kd_t = (k * kdec).T.astype(BF16)
        s_scr[...] = sdec * s + _dot(kd_t, v)
        o = o * lax.rsqrt(jnp.mean(o * o, axis=-1, keepdims=True) + EPS)
        g = pg_ref[rows, :]
        o_ref[rows, :] = (g * jax.nn.sigmoid(g) * o).astype(o_ref.dtype)

    @pl.when(step == pl.num_programs(2) - 1)
    def _():
        sout_ref[...] = s_scr[...]


def _ret_tables(chunk):
    lg = jnp.log(1.0 - 2.0 ** (-5.0 - jnp.arange(RET_HEADS, dtype=F32)))[:, None]
    i = jnp.arange(chunk, dtype=F32)
    diff = i[:, None] - i[None, :]
    dmat = jnp.where(diff >= 0, jnp.exp(jnp.maximum(diff, 0.0)[None] * lg[:, :, None]), 0.0)
    q_dec = jnp.exp((i + 1.0)[None] * lg)
    k_dec = jnp.exp((chunk - 1.0 - i)[None] * lg)
    s_dec = jnp.exp(chunk * lg)
    qdec = jnp.broadcast_to(q_dec[:, :, None], (RET_HEADS, chunk, RET_DK))
    kdec = jnp.broadcast_to(k_dec[:, :, None], (RET_HEADS, chunk, RET_DK))
    sdec = jnp.broadcast_to(s_dec[:, :, None], (RET_HEADS, 1, RET_DV))
    return dmat, qdec, kdec, sdec


def _rope_tables(pos):
    half = RET_DK // 2
    inv_freq = 1.0 / (RET_THETA_BASE ** jnp.linspace(0.0, 1.0, half, dtype=F32))
    ang = pos.astype(F32)[:, None] * inv_freq[None, :]
    return jnp.cos(ang), jnp.sin(ang)


def retention(p, row0, n_streams, length, chunk, pos, s0, per_stream_s0, rows_pref=512):
    n_sub = max(1, min(rows_pref // chunk, length // chunk))
    while (length // chunk) % n_sub:
        n_sub -= 1
    rows = n_sub * chunk
    steps = length // rows
    assert row0 % rows == 0
    base = row0 // rows
    cos, sin = _rope_tables(pos)
    dmat, qdec, kdec, sdec = _ret_tables(chunk)
    qb, vb = RET_HEADS, (2 * RET_HEADS * RET_DK) // RET_DV

    def prow(b, i):
        return base + b * steps + i

    s0_map = (lambda b, h, i: (b, h, 0, 0)) if per_stream_s0 else (lambda b, h, i: (0, h, 0, 0))
    return pl.pallas_call(
        functools.partial(_ret_kernel, chunk=chunk, n_sub=n_sub),
        grid=(n_streams, RET_HEADS, steps),
        in_specs=[
            pl.BlockSpec((rows, RET_DK), lambda b, h, i: (prow(b, i), h)),
            pl.BlockSpec((rows, RET_DK), lambda b, h, i: (prow(b, i), qb + h)),
            pl.BlockSpec((rows, RET_DV), lambda b, h, i: (prow(b, i), vb + h)),
            pl.BlockSpec((rows, RET_DV), lambda b, h, i: (prow(b, i), vb + RET_HEADS + h)),
            pl.BlockSpec((rows, RET_DK // 2), lambda b, h, i: (i, 0)),
            pl.BlockSpec((rows, RET_DK // 2), lambda b, h, i: (i, 0)),
            pl.BlockSpec((None, chunk, chunk), lambda b, h, i: (h, 0, 0)),
            pl.BlockSpec((None, chunk, RET_DK), lambda b, h, i: (h, 0, 0)),
            pl.BlockSpec((None, chunk, RET_DK), lambda b, h, i: (h, 0, 0)),
            pl.BlockSpec((None, 1, RET_DV), lambda b, h, i: (h, 0, 0)),
            pl.BlockSpec((None, None, RET_DK, RET_DV), s0_map),
        ],
        out_specs=[
            pl.BlockSpec((rows, RET_DV), lambda b, h, i: (b * steps + i, h)),
            pl.BlockSpec((None, None, RET_DK, RET_DV), lambda b, h, i: (b, h, 0, 0)),
        ],
        out_shape=[
            jax.ShapeDtypeStruct((n_streams * length, RET_HEADS * RET_DV), BF16),
            jax.ShapeDtypeStruct((n_streams, RET_HEADS, RET_DK, RET_DV), F32),
        ],
        scratch_shapes=[pltpu.VMEM((RET_DK, RET_DV), F32)],
        compiler_params=_params(3),
        name="retention",
    )(p, p, p, p, cos, sin, dmat, qdec, kdec, sdec, s0)


def _router_kernel(h_ref, g_ref, w_ref, b_ref, c_ref):
    xn = _rms(h_ref[...], g_ref[...])
    logits = jnp.dot(xn, w_ref[...], precision=lax.Precision.HIGHEST,
                     preferred_element_type=F32) + b_ref[...]
    lane = lax.broadcasted_iota(jnp.int32, logits.shape, 1).astype(F32)
    neg = jnp.float32(-jnp.inf)
    logits = jnp.where(lane < N_EXP, logits, neg)
    m1 = jnp.max(logits, axis=-1, keepdims=True)
    i1 = jnp.min(jnp.where(logits == m1, lane, float(LANES)), axis=-1, keepdims=True)
    rest = jnp.where(lane == i1, neg, logits)
    m2 = jnp.max(rest, axis=-1, keepdims=True)
    i2 = jnp.min(jnp.where(rest == m2, lane, float(LANES)), axis=-1, keepdims=True)
    e2 = jnp.exp(m2 - m1)
    den = 1.0 + e2
    c_ref[...] = jnp.where(lane == i1, 1.0 / den, 0.0) + jnp.where(lane == i2, e2 / den, 0.0)


def router(h, g, w_r, b_r, tm_pref=1024):
    m, d = h.shape
    tm = _tile(m, tm_pref)
    w_pad = jnp.zeros((d, LANES), F32).at[:, :N_EXP].set(w_r.astype(F32))
    b_pad = jnp.zeros((1, LANES), F32).at[0, :N_EXP].set(b_r.astype(F32))
    return pl.pallas_call(
        _router_kernel,
        grid=(m // tm,),
        in_specs=[pl.BlockSpec((tm, d), lambda i: (i, 0)),
                  pl.BlockSpec((1, d), lambda i: (0, 0)),
                  pl.BlockSpec((d, LANES), lambda i: (0, 0)),
                  pl.BlockSpec((1, LANES), lambda i: (0, 0))],
        out_specs=pl.BlockSpec((tm, LANES), lambda i: (i, 0)),
        out_shape=jax.ShapeDtypeStruct((m, LANES), F32),
        compiler_params=_params(1),
        name="router",
    )(h, g.reshape(1, d), w_pad, b_pad)


def _moe_kernel(x_ref, c_ref, wg_ref, wu_ref, wo_ref, h_ref, *rest, n_norm):
    g_refs = rest[:n_norm]
    ho_ref = rest[n_norm]
    xn_refs = rest[n_norm + 1:n_norm + 1 + n_norm]
    acc = rest[-1]
    e = pl.program_id(1)

    @pl.when(e == 0)
    def _():
        acc[...] = h_ref[...]

    comb = c_ref[...]
    lane = lax.broadcasted_iota(jnp.int32, comb.shape, 1)
    ce = jnp.sum(jnp.where(lane == e, comb, 0.0), axis=-1, keepdims=True)

    @pl.when(jnp.max(ce) > 0.0)
    def _():
        x = x_ref[...]
        gate = _dot(x, wg_ref[...])
        up = _dot(x, wu_ref[...])
        act = (gate * jax.nn.sigmoid(gate) * up).astype(BF16)
        acc[...] += ce * _dot(act, wo_ref[...])

    @pl.when(e == pl.num_programs(1) - 1)
    def _():
        h = acc[...]
        ho_ref[...] = h
        if n_norm:
            y = h * lax.rsqrt(jnp.mean(h * h, axis=-1, keepdims=True) + EPS)
            for g_ref, xn_ref in zip(g_refs, xn_refs):
                xn_ref[...] = (y * g_ref[...]).astype(xn_ref.dtype)


def moe(x, comb, w_in, w_out, h, gains=(), tm_pref=512):
    m, d = x.shape
    n_exp, _, f2 = w_in.shape
    f = f2 // 2
    tm = _tile(m, tm_pref)
    n_norm = len(gains)
    row = pl.BlockSpec((tm, d), lambda i, e: (i, 0))
    gspec = pl.BlockSpec((1, d), lambda i, e: (0, 0))
    outs = pl.pallas_call(
        functools.partial(_moe_kernel, n_norm=n_norm),
        grid=(m // tm, n_exp),
        in_specs=[row,
                  pl.BlockSpec((tm, LANES), lambda i, e: (i, 0)),
                  pl.BlockSpec((None, d, f), lambda i, e: (e, 0, 0)),
                  pl.BlockSpec((None, d, f), lambda i, e: (e, 0, 1)),
                  pl.BlockSpec((None, f, d), lambda i, e: (e, 0, 0)),
                  row] + [gspec] * n_norm,
        out_specs=[row] * (1 + n_norm),
        out_shape=[jax.ShapeDtypeStruct((m, d), F32)]
        + [jax.ShapeDtypeStruct((m, d), BF16)] * n_norm,
        scratch_shapes=[pltpu.VMEM((tm, d), F32)],
        compiler_params=_params(2),
        name="moe",
    )(x, comb, w_in, w_in, w_out, h, *[g.reshape(1, d) for g in gains])
    return outs[0], tuple(outs[1:])


def _moe_sparse_kernel(x_ref, c_ref, wg_ref, wu_ref, wo_ref, h_ref, *rest, n_norm, cap):
    g_refs = rest[:n_norm]
    ho_ref = rest[n_norm]
    xn_refs = rest[n_norm + 1:n_norm + 1 + n_norm]
    acc, rankt_ref, combt_ref = rest[-3:]
    e = pl.program_id(1)
    tm = x_ref.shape[0]

    @pl.when(e == 0)
    def _():
        acc[...] = h_ref[...]
        sel = c_ref[...] > 0.0
        r = lax.broadcasted_iota(jnp.int32, (tm, tm), 0)
        c = lax.broadcasted_iota(jnp.int32, (tm, tm), 1)
        before = jnp.where(c < r, 1.0, 0.0).astype(BF16)
        rank = jnp.where(sel, _dot(before, jnp.where(sel, 1.0, 0.0).astype(BF16)), -1.0)
        rankt_ref[...] = rank.T
        combt_ref[...] = c_ref[...].T

    rk_row = rankt_ref[pl.ds(e, 1), :]
    gate_row = combt_ref[pl.ds(e, 1), :]
    n_routed = jnp.max(rk_row) + 1.0
    for ch in range(-(-tm // cap)):
        @pl.when(n_routed > float(ch * cap))
        def _():
            i_col = (lax.broadcasted_iota(jnp.int32, (cap, 1), 0) + ch * cap).astype(F32)
            hit = rk_row == i_col
            pick = jnp.where(hit, 1.0, 0.0).astype(BF16)
            gate_c = jnp.sum(jnp.where(hit, gate_row, 0.0), axis=-1, keepdims=True)
            xs = _dot(pick, x_ref[...]).astype(BF16)
            gate = _dot(xs, wg_ref[...])
            up = _dot(xs, wu_ref[...])
            act = (gate * jax.nn.sigmoid(gate) * up).astype(BF16)
            y = gate_c * _dot(act, wo_ref[...])
            hi = y.astype(BF16)
            lo = (y - hi.astype(F32)).astype(BF16)
            tn = (((0,), (0,)), ((), ()))
            acc[...] += (lax.dot_general(pick, hi, tn, preferred_element_type=F32)
                         + lax.dot_general(pick, lo, tn, preferred_element_type=F32))

    @pl.when(e == pl.num_programs(1) - 1)
    def _():
        h = acc[...]
        ho_ref[...] = h
        if n_norm:
            y = h * lax.rsqrt(jnp.mean(h * h, axis=-1, keepdims=True) + EPS)
            for g_ref, xn_ref in zip(g_refs, xn_refs):
                xn_ref[...] = (y * g_ref[...]).astype(xn_ref.dtype)


def moe_sparse(x, comb, w_in, w_out, h, gains=(), tm=MOE_TILE, cap=MOE_CHUNK):
    m, d = x.shape
    n_exp, _, f2 = w_in.shape
    f = f2 // 2
    assert m % tm == 0
    n_norm = len(gains)
    row = pl.BlockSpec((tm, d), lambda i, e: (i, 0))
    gspec = pl.BlockSpec((1, d), lambda i, e: (0, 0))
    outs = pl.pallas_call(
        functools.partial(_moe_sparse_kernel, n_norm=n_norm, cap=cap),
        grid=(m // tm, n_exp),
        in_specs=[row,
                  pl.BlockSpec((tm, LANES), lambda i, e: (i, 0)),
                  pl.BlockSpec((None, d, f), lambda i, e: (e, 0, 0)),
                  pl.BlockSpec((None, d, f), lambda i, e: (e, 0, 1)),
                  pl.BlockSpec((None, f, d), lambda i, e: (e, 0, 0)),
                  row] + [gspec] * n_norm,
        out_specs=[row] * (1 + n_norm),
        out_shape=[jax.ShapeDtypeStruct((m, d), F32)]
        + [jax.ShapeDtypeStruct((m, d), BF16)] * n_norm,
        scratch_shapes=[pltpu.VMEM((tm, d), F32), pltpu.VMEM((LANES, tm), F32),
                        pltpu.VMEM((LANES, tm), F32)],
        compiler_params=_params(2),
        name="moe_sparse",
    )(x, comb, w_in, w_in, w_out, h, *[g.reshape(1, d) for g in gains])
    return outs[0], tuple(outs[1:])


def _softplus2(z):
    return jnp.maximum(jnp.log2(1.0 + jnp.exp2(jnp.minimum(z, 126.0))), z)


def _suffix_ones(n, keys_on_sublanes):
    r = lax.broadcasted_iota(jnp.int32, (n, n), 0)
    c = lax.broadcasted_iota(jnp.int32, (n, n), 1)
    return jnp.where((c >= r) if keys_on_sublanes else (r >= c), 1.0, 0.0).astype(BF16)


def _sb_prompt_kernel(q_ref, k_ref, vt_ref, o_ref, acc_ref, z_buf, d_buf, *, tq, bk, n_frame_blocks):
    t = pl.program_id(2)
    rows = SB_GROUP * tq
    qs = jnp.concatenate([q_ref[:, g * SB_DH:(g + 1) * SB_DH] for g in range(SB_GROUP)], axis=0)
    qt = qs.T
    qfrm = t * tq + lax.rem(lax.broadcasted_iota(jnp.int32, (1, rows), 1), tq)
    trit = _suffix_ones(bk, True)
    per_tile = tq // bk
    n_before = t * per_tile

    krow = lax.broadcasted_iota(jnp.int32, (bk, 1), 0)
    js = [n_before + per_tile - 1 - m for m in range(per_tile)]
    zs = [_dot(k_ref[j], qt) for j in js]
    valids = [(j * bk + krow) < qfrm for j in js]
    crun = jnp.zeros((1, rows), F32)
    ds = []
    for z, valid in zip(zs, valids):
        cum = _dot(trit, jnp.where(valid, _softplus2(z), 0.0).astype(BF16))
        ds.append(cum - crun)
        crun = crun - cum[0:1, :]
    acc = jnp.zeros((SB_DH, rows), F32)
    for j, z, valid, d in zip(js, zs, valids, ds):
        acc = acc + _dot(vt_ref[j], jnp.where(valid, jnp.exp2(z - d), 0.0).astype(BF16))
    acc_ref[...] = acc

    def blk(u):
        return jnp.where(u < n_before, n_before - 1 - u,
                         jnp.minimum(n_frame_blocks + u - n_before, n_frame_blocks + 1))

    def stage_z(u, slot):
        z_buf[slot] = _dot(k_ref[blk(u)], qt)

    def stage_d(slot, crun):
        cum = _dot(trit, _softplus2(z_buf[slot]).astype(BF16))
        d_buf[slot] = cum - crun
        return crun - cum[0:1, :]

    def stage_acc(u, slot):
        a = jnp.exp2(z_buf[slot] - d_buf[slot])
        acc_ref[...] += _dot(vt_ref[blk(u)], a.astype(BF16))

    stage_z(0, 0)
    stage_z(1, 1)
    crun = stage_d(0, crun)

    def steady(i, crun):
        for r in range(3):
            stage_z(3 * i + r + 2, (r + 2) % 3)
            stage_acc(3 * i + r, r)
            crun = stage_d((r + 1) % 3, crun)
        return crun

    lax.fori_loop(0, (n_before + 1 + 2) // 3, steady, crun)
    acc = acc_ref[...].T
    for g in range(SB_GROUP):
        o_ref[:, g * SB_DH:(g + 1) * SB_DH] = acc[g * tq:(g + 1) * tq].astype(o_ref.dtype)


def sb_attention_prompt(q, kb, vt, bk, tq_pref=512):
    b, lq, _ = q.shape
    nblk = kb.shape[2]
    tq = tq_pref if lq % tq_pref == 0 else bk
    assert lq % tq == 0 and tq % bk == 0 and nblk == lq // bk + 2
    gw = SB_GROUP * SB_DH
    rows = SB_GROUP * tq
    return pl.pallas_call(
        functools.partial(_sb_prompt_kernel, tq=tq, bk=bk, n_frame_blocks=lq // bk),
        grid=(b, SB_KV_HEADS, lq // tq),
        in_specs=[pl.BlockSpec((None, tq, gw), lambda bi, h, t: (bi, t, h)),
                  pl.BlockSpec((None, None, nblk, bk, SB_DH), lambda bi, h, t: (bi, h, 0, 0, 0)),
                  pl.BlockSpec((None, None, nblk, SB_DH, bk), lambda bi, h, t: (bi, h, 0, 0, 0))],
        out_specs=pl.BlockSpec((None, tq, gw), lambda bi, h, t: (bi, t, h)),
        out_shape=jax.ShapeDtypeStruct((b, lq, SB_HEADS * SB_DH), BF16),
        scratch_shapes=[pltpu.VMEM((SB_DH, rows), F32),
                        pltpu.VMEM((3, bk, rows), F32),
                        pltpu.VMEM((3, bk, rows), F32)],
        compiler_params=_params(3),
        name="sb_attention_prompt",
    )(q, kb, vt)


def _sb_sample_kernel(q_ref, kc_ref, vc_ref, kn_ref, vn_ref, o_ref, *, bk):
    ds = q_ref.shape[0]
    n_cache = kc_ref.shape[0]
    rows = SB_GROUP * ds
    qs = jnp.concatenate([q_ref[:, g * SB_DH:(g + 1) * SB_DH] for g in range(SB_GROUP)], axis=0)
    qrow = lax.rem(lax.broadcasted_iota(jnp.int32, (rows, 1), 0), ds)
    new_valid = lax.broadcasted_iota(jnp.int32, (1, ds), 1) < qrow

    spans = [(kn_ref, vn_ref, 0, ds, new_valid)]
    whole = n_cache // bk
    if n_cache % bk:
        spans.append((kc_ref, vc_ref, whole * bk, n_cache % bk, None))
    spans += [(kc_ref, vc_ref, j * bk, bk, None) for j in reversed(range(whole))]

    zs = [lax.dot_general(qs, kr[st:st + sz, :].astype(BF16), (((1,), (1,)), ((), ())),
                          preferred_element_type=F32) for kr, _, st, sz, _ in spans]
    tris = {sz: _suffix_ones(sz, False) for sz in {sp[3] for sp in spans}}
    crun = jnp.zeros((rows, 1), F32)
    dds = []
    for z, (_, _, _, sz, valid) in zip(zs, spans):
        sp = _softplus2(z)
        if valid is not None:
            sp = jnp.where(valid, sp, 0.0)
        cum = _dot(sp.astype(BF16), tris[sz])
        dds.append(cum - crun)
        crun = crun - cum[:, 0:1]
    acc = jnp.zeros((rows, SB_DH), F32)
    for z, d, (_, vr, st, sz, valid) in zip(zs, dds, spans):
        a = jnp.exp2(z - d)
        if valid is not None:
            a = jnp.where(valid, a, 0.0)
        acc = acc + _dot(a.astype(BF16), vr[st:st + sz, :].astype(BF16))
    for g in range(SB_GROUP):
        o_ref[:, g * SB_DH:(g + 1) * SB_DH] = acc[g * ds:(g + 1) * ds].astype(o_ref.dtype)


def sb_attention_sample(q, cache_k, cache_v, k_new, v_new, bk=256):
    b, ds, _ = q.shape
    n_cache = cache_k.shape[1]
    gw = SB_GROUP * SB_DH
    cache = pl.BlockSpec((None, n_cache, SB_DH), lambda bi, h: (bi, 0, h))
    new = pl.BlockSpec((None, ds, SB_DH), lambda bi, h: (bi, 0, h))
    qo = pl.BlockSpec((None, ds, gw), lambda bi, h: (bi, 0, h))
    return pl.pallas_call(
        functools.partial(_sb_sample_kernel, bk=bk),
        grid=(b, SB_KV_HEADS),
        in_specs=[qo, cache, cache, new, new],
        out_specs=qo,
        out_shape=jax.ShapeDtypeStruct((b, ds, SB_HEADS * SB_DH), BF16),
        compiler_params=_params(2),
        name="sb_attention_sample",
    )(q, cache_k, cache_v, k_new, v_new)


def kernel(x_prompt, x_sample, state_ret, cache_k, cache_v, meta_tokens, norm_mix, norm_ffn, w_ret_in, w_ret_out, norm_kv, w_kv, k_norm, w_sb_q, q_norm, w_sb_o, w_ffn_in, w_ffn_out, w_router, b_router, w_exp_in, w_exp_out):
    bp, seq, d = x_prompt.shape
    bs, ds, _ = x_sample.shape
    n_a = w_ret_in.shape[0]
    depth = norm_mix.shape[0]
    p0_s = cache_k.shape[1]
    m_main, m_s = bp * seq, bs * ds

    w_ret_in, w_ret_out, w_kv, w_sb_q, w_sb_o, w_ffn_in, w_ffn_out, w_exp_in, w_exp_out = (
        w.astype(BF16) for w in (w_ret_in, w_ret_out, w_kv, w_sb_q, w_sb_o, w_ffn_in, w_ffn_out,
                                 w_exp_in, w_exp_out))

    def channel(l, h, xn, gains):
        if l % 2 == 0:
            act = swiglu_in(xn, w_ffn_in[l // 2])
            return out_proj(act, w_ffn_out[l // 2], h, gains)
        comb = router(h, norm_ffn[l], w_router[l // 2], b_router[l // 2])
        experts = moe_sparse if h.shape[0] % MOE_TILE == 0 else moe
        return experts(xn, comb, w_exp_in[l // 2], w_exp_out[l // 2], h, gains)

    def next_gains(l):
        if l + 1 == depth:
            return ()
        if l + 1 == n_a:
            return (norm_kv, norm_mix[l + 1])
        return (norm_mix[l + 1],)

    h_s = jnp.concatenate([x_sample.reshape(m_s, d), meta_tokens.astype(x_prompt.dtype)], axis=0)
    h_m = x_prompt.reshape(m_main, d)
    xn_s = rmsnorm(h_s, norm_mix[0])
    xn_m = rmsnorm(h_m, norm_mix[0])
    pos_meta = jnp.arange(N_META)
    pos_s = p0_s + jnp.arange(ds)
    pos_m = N_META + jnp.arange(seq)
    zero_state = jnp.zeros((1, RET_HEADS, RET_DK, RET_DV), F32)

    ret_p, ret_s = [], []
    for l in range(n_a):
        p_s = matmul(xn_s, w_ret_in[l])
        o_s, st_s = retention(p_s, 0, bs, ds, ds, pos_s, state_ret[l].astype(F32), True)
        o_meta, st_meta = retention(p_s, m_s, 1, N_META, N_META, pos_meta, zero_state, False)
        p_m = matmul(xn_m, w_ret_in[l])
        o_m, st_m = retention(p_m, 0, bp, seq, RET_BLOCK, pos_m, st_meta, False)
        ret_s.append(st_s.astype(state_ret.dtype))
        ret_p.append(st_m.astype(x_prompt.dtype))
        h_s, (xn_s,) = out_proj(jnp.concatenate([o_s, o_meta], axis=0), w_ret_out[l], h_s,
                                (norm_ffn[l],))
        h_m, (xn_m,) = out_proj(o_m, w_ret_out[l], h_m, (norm_ffn[l],))
        h_s, xns_s = channel(l, h_s, xn_s, next_gains(l))
        h_m, xns_m = channel(l, h_m, xn_m, next_gains(l))
        if l + 1 < n_a:
            (xn_s,), (xn_m,) = xns_s, xns_m

    (xkv_s, xq_s), (xkv_m, xq_m) = xns_s, xns_m
    k_sm, v_sm = kv_proj(xkv_s, w_kv, k_norm)
    k_m, v_m, kb, vt = kv_proj_blocks(xkv_m, w_kv, k_norm, bp, SB_BLOCK, 2)
    nkv = SB_KV_HEADS * SB_DH
    k_meta, v_meta = k_sm[m_s:], v_sm[m_s:]

    def extra_blocks(meta_rows):
        blk = jnp.pad(meta_rows.astype(BF16), ((SB_BLOCK - N_META, 0), (0, 0)))
        blk = blk.reshape(SB_BLOCK, SB_KV_HEADS, SB_DH).transpose(1, 0, 2)
        both = jnp.stack([blk, jnp.zeros_like(blk)], axis=1)
        return jnp.broadcast_to(both[None], (bp,) + both.shape)

    n_fb = seq // SB_BLOCK
    kb = lax.dynamic_update_slice(kb, extra_blocks(k_meta), (0, 0, n_fb, 0, 0))
    vt = lax.dynamic_update_slice(vt, extra_blocks(v_meta).transpose(0, 1, 2, 4, 3), (0, 0, n_fb, 0, 0))

    def with_meta(meta_rows, frames):
        meta_b = jnp.broadcast_to(meta_rows[None], (bp, N_META, nkv))
        return jnp.concatenate([meta_b, frames.reshape(bp, seq, nkv)], axis=1)

    k_p = with_meta(k_meta, k_m)
    v_p = with_meta(v_meta, v_m)
    k_s = k_sm[:m_s].reshape(bs, ds, nkv)
    v_s = v_sm[:m_s].reshape(bs, ds, nkv)
    cache_k2 = cache_k.reshape(bs, p0_s, nkv)
    cache_v2 = cache_v.reshape(bs, p0_s, nkv)

    h_s, xn_s = h_s[:m_s], xq_s[:m_s]
    xn_m = xq_m
    for l in range(n_a, depth):
        j = l - n_a
        q_s = q_proj(xn_s, w_sb_q[j], q_norm[j]).reshape(bs, ds, d)
        q_m = q_proj(xn_m, w_sb_q[j], q_norm[j]).reshape(bp, seq, d)
        o_s = sb_attention_sample(q_s, cache_k2, cache_v2, k_s.astype(cache_k.dtype),
                                  v_s.astype(cache_v.dtype), SB_BLOCK).reshape(m_s, d)
        o_m = sb_attention_prompt(q_m, kb, vt, SB_BLOCK).reshape(m_main, d)
        h_s, (xn_s,) = out_proj(o_s, w_sb_o[j], h_s, (norm_ffn[l],))
        h_m, (xn_m,) = out_proj(o_m, w_sb_o[j], h_m, (norm_ffn[l],))
        h_s, xns_s = channel(l, h_s, xn_s, next_gains(l))
        h_m, xns_m = channel(l, h_m, xn_m, next_gains(l))
        if l + 1 < depth:
            (xn_s,), (xn_m,) = xns_s, xns_m

    y_prompt = h_m.reshape(bp, seq, d)
    y_sample = h_s.reshape(bs, ds, d)
    kv4 = lambda t: t.reshape(t.shape[0], t.shape[1], SB_KV_HEADS, SB_DH)
    return (y_prompt, y_sample, jnp.stack(ret_p), kv4(k_p), kv4(v_p), jnp.stack(ret_s),
            kv4(k_s), kv4(v_s))
```

```python
import functools

import jax
import jax.numpy as jnp
from jax import lax
from jax.experimental import pallas as pl
from jax.experimental.pallas import tpu as pltpu

D_MODEL = 1024
N_META = 16
RET_BLOCK = 256
SB_BLOCK = 256
MOE_TILE = 1024
MOE_HALF = 512
MOE_CHUNK = 160
RET_HEADS = 4
RET_DK = 256
RET_DV = 512
RET_IN = RET_HEADS * (2 * RET_DK + 2 * RET_DV)
RET_THETA_BASE = 10000.0
SB_HEADS = 8
SB_KV_HEADS = 4
SB_GROUP = SB_HEADS // SB_KV_HEADS
SB_DH = 128
N_EXP = 8
EPS = 1e-6
LOG2E = 1.4426950408889634

LANES = 128
BF16_ROWS = 16
VMEM_LIMIT = 56 * 1024 * 1024

F32 = jnp.float32
BF16 = jnp.bfloat16


def _params(n_axes):
    return pltpu.CompilerParams(dimension_semantics=("arbitrary",) * n_axes,
                                vmem_limit_bytes=VMEM_LIMIT)


def _tile(n, pref, mult=BF16_ROWS):
    if n <= pref:
        return n
    t = (pref // mult) * mult
    while t >= mult:
        if n % t == 0:
            return t
        t -= mult
    return n


def _dot(a, b):
    return jnp.dot(a, b, preferred_element_type=F32)


def _rms(x, g):
    return (x * lax.rsqrt(jnp.mean(x * x, axis=-1, keepdims=True) + EPS)) * g


def _rmsnorm_kernel(x_ref, g_ref, o_ref):
    o_ref[...] = _rms(x_ref[...], g_ref[...]).astype(o_ref.dtype)


def rmsnorm(x, g, tm_pref=1024):
    m, d = x.shape
    tm = _tile(m, tm_pref)
    return pl.pallas_call(
        _rmsnorm_kernel,
        grid=(m // tm,),
        in_specs=[pl.BlockSpec((tm, d), lambda i: (i, 0)),
                  pl.BlockSpec((1, d), lambda i: (0, 0))],
        out_specs=pl.BlockSpec((tm, d), lambda i: (i, 0)),
        out_shape=jax.ShapeDtypeStruct((m, d), BF16),
        compiler_params=_params(1),
        name="rmsnorm",
    )(x, g.reshape(1, d))


def _mm_kernel(x_ref, w_ref, o_ref):
    o_ref[...] = _dot(x_ref[...], w_ref[...]).astype(o_ref.dtype)


def matmul(x, w, out_dtype=F32, tm_pref=1024, tn_pref=1024):
    m, k = x.shape
    n = w.shape[1]
    tm = _tile(m, tm_pref)
    tn = _tile(n, tn_pref, LANES)
    return pl.pallas_call(
        _mm_kernel,
        grid=(n // tn, m // tm),
        in_specs=[pl.BlockSpec((tm, k), lambda j, i: (i, 0)),
                  pl.BlockSpec((k, tn), lambda j, i: (0, j))],
        out_specs=pl.BlockSpec((tm, tn), lambda j, i: (i, j)),
        out_shape=jax.ShapeDtypeStruct((m, n), out_dtype),
        compiler_params=_params(2),
        name="matmul",
    )(x, w)


def _swiglu_kernel(x_ref, wg_ref, wu_ref, o_ref):
    x = x_ref[...]
    gate = _dot(x, wg_ref[...])
    up = _dot(x, wu_ref[...])
    o_ref[...] = (gate * jax.nn.sigmoid(gate) * up).astype(o_ref.dtype)


def swiglu_in(x, w_in, tm_pref=512, tn_pref=1408):
    m, k = x.shape
    f = w_in.shape[1] // 2
    tm = _tile(m, tm_pref)
    tn = _tile(f, tn_pref, LANES)
    nj = f // tn
    return pl.pallas_call(
        _swiglu_kernel,
        grid=(nj, m // tm),
        in_specs=[pl.BlockSpec((tm, k), lambda j, i: (i, 0)),
                  pl.BlockSpec((k, tn), lambda j, i: (0, j)),
                  pl.BlockSpec((k, tn), lambda j, i: (0, j + nj))],
        out_specs=pl.BlockSpec((tm, tn), lambda j, i: (i, j)),
        out_shape=jax.ShapeDtypeStruct((m, f), BF16),
        compiler_params=_params(2),
        name="swiglu_in",
    )(x, w_in, w_in)


def _out_kernel(x_ref, w_ref, h_ref, *rest, n_norm):
    g_refs = rest[:n_norm]
    ho_ref = rest[n_norm]
    xn_refs = rest[n_norm + 1:]
    h = h_ref[...] + _dot(x_ref[...], w_ref[...])
    ho_ref[...] = h
    if n_norm:
        y = h * lax.rsqrt(jnp.mean(h * h, axis=-1, keepdims=True) + EPS)
        for g_ref, xn_ref in zip(g_refs, xn_refs):
            xn_ref[...] = (y * g_ref[...]).astype(xn_ref.dtype)


def out_proj(x, w, h, gains=(), tm_pref=512):
    m, k = x.shape
    d = w.shape[1]
    tm = _tile(m, tm_pref)
    n_norm = len(gains)
    row = pl.BlockSpec((tm, d), lambda i: (i, 0))
    gspec = pl.BlockSpec((1, d), lambda i: (0, 0))
    outs = pl.pallas_call(
        functools.partial(_out_kernel, n_norm=n_norm),
        grid=(m // tm,),
        in_specs=[pl.BlockSpec((tm, k), lambda i: (i, 0)),
                  pl.BlockSpec((k, d), lambda i: (0, 0)),
                  row] + [gspec] * n_norm,
        out_specs=[row] * (1 + n_norm),
        out_shape=[jax.ShapeDtypeStruct((m, d), F32)]
        + [jax.ShapeDtypeStruct((m, d), BF16)] * n_norm,
        compiler_params=_params(1),
        name="out_proj",
    )(x, w, h, *[g.reshape(1, d) for g in gains])
    return outs[0], tuple(outs[1:])


def _head_norm(y, g, n_heads):
    parts = []
    for hd in range(n_heads):
        yh = y[:, hd * SB_DH:(hd + 1) * SB_DH]
        parts.append(_rms(yh, g))
    return parts


def _qproj_kernel(x_ref, w_ref, g_ref, o_ref):
    y = _dot(x_ref[...], w_ref[...])
    g = g_ref[...]
    for hd, yh in enumerate(_head_norm(y, g, SB_HEADS)):
        o_ref[:, hd * SB_DH:(hd + 1) * SB_DH] = (yh * (SB_DH ** -0.5 * LOG2E)).astype(o_ref.dtype)


def q_proj(x, w, g, tm_pref=512):
    m, k = x.shape
    n = w.shape[1]
    tm = _tile(m, tm_pref)
    return pl.pallas_call(
        _qproj_kernel,
        grid=(m // tm,),
        in_specs=[pl.BlockSpec((tm, k), lambda i: (i, 0)),
                  pl.BlockSpec((k, n), lambda i: (0, 0)),
                  pl.BlockSpec((1, SB_DH), lambda i: (0, 0))],
        out_specs=pl.BlockSpec((tm, n), lambda i: (i, 0)),
        out_shape=jax.ShapeDtypeStruct((m, n), BF16),
        compiler_params=_params(1),
        name="q_proj",
    )(x, w, g.reshape(1, SB_DH))


def _kvproj_kernel(x_ref, w_ref, g_ref, k_ref, v_ref):
    y = _dot(x_ref[...], w_ref[...])
    nk = SB_KV_HEADS * SB_DH
    g = g_ref[...]
    for hd, yh in enumerate(_head_norm(y[:, :nk], g, SB_KV_HEADS)):
        k_ref[:, hd * SB_DH:(hd + 1) * SB_DH] = yh
    v_ref[...] = y[:, nk:]


def kv_proj(x, w, g, tm_pref=512):
    m, k = x.shape
    n = w.shape[1]
    nk = n // 2
    tm = _tile(m, tm_pref)
    return pl.pallas_call(
        _kvproj_kernel,
        grid=(m // tm,),
        in_specs=[pl.BlockSpec((tm, k), lambda i: (i, 0)),
                  pl.BlockSpec((k, n), lambda i: (0, 0)),
                  pl.BlockSpec((1, SB_DH), lambda i: (0, 0))],
        out_specs=[pl.BlockSpec((tm, nk), lambda i: (i, 0))] * 2,
        out_shape=[jax.ShapeDtypeStruct((m, nk), F32)] * 2,
        compiler_params=_params(1),
        name="kv_proj",
    )(x, w, g.reshape(1, SB_DH))


def _kvproj_blocks_kernel(x_ref, w_ref, g_ref, k_ref, v_ref, kb_ref, vt_ref, *, bk):
    y = _dot(x_ref[...], w_ref[...])
    nk = SB_KV_HEADS * SB_DH
    n_sub = x_ref.shape[0] // bk
    g = g_ref[...]
    for hd, yh in enumerate(_head_norm(y[:, :nk], g, SB_KV_HEADS)):
        k_ref[:, hd * SB_DH:(hd + 1) * SB_DH] = yh
        for r in range(n_sub):
            kb_ref[hd, r] = yh[r * bk:(r + 1) * bk].astype(BF16)
    v_ref[...] = y[:, nk:]
    for hd in range(SB_KV_HEADS):
        vh = y[:, nk + hd * SB_DH:nk + (hd + 1) * SB_DH]
        for r in range(n_sub):
            vt_ref[hd, r] = vh[r * bk:(r + 1) * bk].T.astype(BF16)


def kv_proj_blocks(x, w, g, batch, bk, n_extra, tm_pref=512):
    m, k = x.shape
    n = w.shape[1]
    nk = n // 2
    length = m // batch
    tm = tm_pref if length % tm_pref == 0 else bk
    assert length % tm == 0 and tm % bk == 0
    steps, n_sub = length // tm, tm // bk
    nblk = length // bk + n_extra
    return pl.pallas_call(
        functools.partial(_kvproj_blocks_kernel, bk=bk),
        grid=(batch, steps),
        in_specs=[pl.BlockSpec((tm, k), lambda b, i: (b * steps + i, 0)),
                  pl.BlockSpec((k, n), lambda b, i: (0, 0)),
                  pl.BlockSpec((1, SB_DH), lambda b, i: (0, 0))],
        out_specs=[pl.BlockSpec((tm, nk), lambda b, i: (b * steps + i, 0))] * 2
        + [pl.BlockSpec((None, SB_KV_HEADS, n_sub, bk, SB_DH), lambda b, i: (b, 0, i, 0, 0)),
           pl.BlockSpec((None, SB_KV_HEADS, n_sub, SB_DH, bk), lambda b, i: (b, 0, i, 0, 0))],
        out_shape=[jax.ShapeDtypeStruct((m, nk), F32)] * 2
        + [jax.ShapeDtypeStruct((batch, SB_KV_HEADS, nblk, bk, SB_DH), BF16),
           jax.ShapeDtypeStruct((batch, SB_KV_HEADS, nblk, SB_DH, bk), BF16)],
        compiler_params=_params(2),
        name="kv_proj_blocks",
    )(x, w, g.reshape(1, SB_DH))


def _rope(x, cos, sin):
    half = RET_DK // 2
    x1, x2 = x[:, :half], x[:, half:]
    return jnp.concatenate([x1 * cos - x2 * sin, x2 * cos + x1 * sin], axis=1)


def _ret_kernel(pq_ref, pk_ref, pv_ref, pg_ref, cos_ref, sin_ref, dmat_ref, qdec_ref, kdec_ref,
                sdec_ref, s0_ref, o_ref, sout_ref, s_scr, *, chunk, n_sub):
    step = pl.program_id(2)

    @pl.when(step == 0)
    def _():
        s_scr[...] = s0_ref[...]

    dmat = dmat_ref[...]
    qdec = qdec_ref[...]
    kdec = kdec_ref[...]
    sdec = sdec_ref[...]
    for c in range(n_sub):
        rows = pl.ds(c * chunk, chunk)
        cos = cos_ref[rows, :]
        sin = sin_ref[rows, :]
        q = _rope(pq_ref[rows, :], cos, sin)
        k = _rope(pk_ref[rows, :], cos, sin) * (RET_DK ** -0.5)
        v = pv_ref[rows, :].astype(BF16)
        s = s_scr[...]
        scores = lax.dot_general(q.astype(BF16), k.astype(BF16), (((1,), (1,)), ((), ())),
                                 preferred_element_type=F32) * dmat
        o = _dot(scores.astype(BF16), v) + _dot((q * qdec).astype(BF16), s.astype(BF16))
        kd_t = (k * kdec).T.astype(BF16)
        s_scr[...] = sdec * s + _dot(kd_t, v)
        o = o * lax.rsqrt(jnp.mean(o * o, axis=-1, keepdims=True) + EPS)
        g = pg_ref[rows, :]
        o_ref[rows, :] = (g * jax.nn.sigmoid(g) * o).astype(o_ref.dtype)

    @pl.when(step == pl.num_programs(2) - 1)
    def _():
        sout_ref[...] = s_scr[...]


def _ret_tables(chunk):
    lg = jnp.log(1.0 - 2.0 ** (-5.0 - jnp.arange(RET_HEADS, dtype=F32)))[:, None]
    i = jnp.arange(chunk, dtype=F32)
    diff = i[:, None] - i[None, :]
    dmat = jnp.where(diff >= 0, jnp.exp(jnp.maximum(diff, 0.0)[None] * lg[:, :, None]), 0.0)
    q_dec = jnp.exp((i + 1.0)[None] * lg)
    k_dec = jnp.exp((chunk - 1.0 - i)[None] * lg)
    s_dec = jnp.exp(chunk * lg)
    qdec = jnp.broadcast_to(q_dec[:, :, None], (RET_HEADS, chunk, RET_DK))
    kdec = jnp.broadcast_to(k_dec[:, :, None], (RET_HEADS, chunk, RET_DK))
    sdec = jnp.broadcast_to(s_dec[:, :, None], (RET_HEADS, 1, RET_DV))
    return dmat, qdec, kdec, sdec


def _rope_tables(pos):
    half = RET_DK // 2
    inv_freq = 1.0 / (RET_THETA_BASE ** jnp.linspace(0.0, 1.0, half, dtype=F32))
    ang = pos.astype(F32)[:, None] * inv_freq[None, :]
    return jnp.cos(ang), jnp.sin(ang)


def retention(p, row0, n_streams, length, chunk, pos, s0, per_stream_s0, rows_pref=512):
    n_sub = max(1, min(rows_pref // chunk, length // chunk))
    while (length // chunk) % n_sub:
        n_sub -= 1
    rows = n_sub * chunk
    steps = length // rows
    assert row0 % rows == 0
    base = row0 // rows
    cos, sin = _rope_tables(pos)
    dmat, qdec, kdec, sdec = _ret_tables(chunk)
    qb, vb = RET_HEADS, (2 * RET_HEADS * RET_DK) // RET_DV

    def prow(b, i):
        return base + b * steps + i

    s0_map = (lambda b, h, i: (b, h, 0, 0)) if per_stream_s0 else (lambda b, h, i: (0, h, 0, 0))
    return pl.pallas_call(
        functools.partial(_ret_kernel, chunk=chunk, n_sub=n_sub),
        grid=(n_streams, RET_HEADS, steps),
        in_specs=[
            pl.BlockSpec((rows, RET_DK), lambda b, h, i: (prow(b, i), h)),
            pl.BlockSpec((rows, RET_DK), lambda b, h, i: (prow(b, i), qb + h)),
            pl.BlockSpec((rows, RET_DV), lambda b, h, i: (prow(b, i), vb + h)),
            pl.BlockSpec((rows, RET_DV), lambda b, h, i: (prow(b, i), vb + RET_HEADS + h)),
            pl.BlockSpec((rows, RET_DK // 2), lambda b, h, i: (i, 0)),
            pl.BlockSpec((rows, RET_DK // 2), lambda b, h, i: (i, 0)),
            pl.BlockSpec((None, chunk, chunk), lambda b, h, i: (h, 0, 0)),
            pl.BlockSpec((None, chunk, RET_DK), lambda b, h, i: (h, 0, 0)),
            pl.BlockSpec((None, chunk, RET_DK), lambda b, h, i: (h, 0, 0)),
            pl.BlockSpec((None, 1, RET_DV), lambda b, h, i: (h, 0, 0)),
            pl.BlockSpec((None, None, RET_DK, RET_DV), s0_map),
        ],
        out_specs=[
            pl.BlockSpec((rows, RET_DV), lambda b, h, i: (b * steps + i, h)),
            pl.BlockSpec((None, None, RET_DK, RET_DV), lambda b, h, i: (b, h, 0, 0)),
        ],
        out_shape=[
            jax.ShapeDtypeStruct((n_streams * length, RET_HEADS * RET_DV), BF16),
            jax.ShapeDtypeStruct((n_streams, RET_HEADS, RET_DK, RET_DV), F32),
        ],
        scratch_shapes=[pltpu.VMEM((RET_DK, RET_DV), F32)],
        compiler_params=_params(3),
        name="retention",
    )(p, p, p, p, cos, sin, dmat, qdec, kdec, sdec, s0)


def _router_kernel(h_ref, g_ref, w_ref, b_ref, c_ref):
    xn = _rms(h_ref[...], g_ref[...])
    logits = jnp.dot(xn, w_ref[...], precision=lax.Precision.HIGHEST,
                     preferred_element_type=F32) + b_ref[...]
    lane = lax.broadcasted_iota(jnp.int32, logits.shape, 1).astype(F32)
    neg = jnp.float32(-jnp.inf)
    logits = jnp.where(lane < N_EXP, logits, neg)
    m1 = jnp.max(logits, axis=-1, keepdims=True)
    i1 = jnp.min(jnp.where(logits == m1, lane, float(LANES)), axis=-1, keepdims=True)
    rest = jnp.where(lane == i1, neg, logits)
    m2 = jnp.max(rest, axis=-1, keepdims=True)
    i2 = jnp.min(jnp.where(rest == m2, lane, float(LANES)), axis=-1, keepdims=True)
    e2 = jnp.exp(m2 - m1)
    den = 1.0 + e2
    c_ref[...] = jnp.where(lane == i1, 1.0 / den, 0.0) + jnp.where(lane == i2, e2 / den, 0.0)


def router(h, g, w_r, b_r, tm_pref=1024):
    m, d = h.shape
    tm = _tile(m, tm_pref)
    w_pad = jnp.zeros((d, LANES), F32).at[:, :N_EXP].set(w_r.astype(F32))
    b_pad = jnp.zeros((1, LANES), F32).at[0, :N_EXP].set(b_r.astype(F32))
    return pl.pallas_call(
        _router_kernel,
        grid=(m // tm,),
        in_specs=[pl.BlockSpec((tm, d), lambda i: (i, 0)),
                  pl.BlockSpec((1, d), lambda i: (0, 0)),
                  pl.BlockSpec((d, LANES), lambda i: (0, 0)),
                  pl.BlockSpec((1, LANES), lambda i: (0, 0))],
        out_specs=pl.BlockSpec((tm, LANES), lambda i: (i, 0)),
        out_shape=jax.ShapeDtypeStruct((m, LANES), F32),
        compiler_params=_params(1),
        name="router",
    )(h, g.reshape(1, d), w_pad, b_pad)


def _moe_kernel(x_ref, c_ref, wg_ref, wu_ref, wo_ref, h_ref, *rest, n_norm):
    g_refs = rest[:n_norm]
    ho_ref = rest[n_norm]
    xn_refs = rest[n_norm + 1:n_norm + 1 + n_norm]
    acc = rest[-1]
    e = pl.program_id(1)

    @pl.when(e == 0)
    def _():
        acc[...] = h_ref[...]

    comb = c_ref[...]
    lane = lax.broadcasted_iota(jnp.int32, comb.shape, 1)
    ce = jnp.sum(jnp.where(lane == e, comb, 0.0), axis=-1, keepdims=True)

    @pl.when(jnp.max(ce) > 0.0)
    def _():
        x = x_ref[...]
        gate = _dot(x, wg_ref[...])
        up = _dot(x, wu_ref[...])
        act = (gate * jax.nn.sigmoid(gate) * up).astype(BF16)
        acc[...] += ce * _dot(act, wo_ref[...])

    @pl.when(e == pl.num_programs(1) - 1)
    def _():
        h = acc[...]
        ho_ref[...] = h
        if n_norm:
            y = h * lax.rsqrt(jnp.mean(h * h, axis=-1, keepdims=True) + EPS)
            for g_ref, xn_ref in zip(g_refs, xn_refs):
                xn_ref[...] = (y * g_ref[...]).astype(xn_ref.dtype)


def moe(x, comb, w_in, w_out, h, gains=(), tm_pref=512):
    m, d = x.shape
    n_exp, _, f2 = w_in.shape
    f = f2 // 2
    tm = _tile(m, tm_pref)
    n_norm = len(gains)
    row = pl.BlockSpec((tm, d), lambda i, e: (i, 0))
    gspec = pl.BlockSpec((1, d), lambda i, e: (0, 0))
    outs = pl.pallas_call(
        functools.partial(_moe_kernel, n_norm=n_norm),
        grid=(m // tm, n_exp),
        in_specs=[row,
                  pl.BlockSpec((tm, LANES), lambda i, e: (i, 0)),
                  pl.BlockSpec((None, d, f), lambda i, e: (e, 0, 0)),
                  pl.BlockSpec((None, d, f), lambda i, e: (e, 0, 1)),
                  pl.BlockSpec((None, f, d), lambda i, e: (e, 0, 0)),
                  row] + [gspec] * n_norm,
        out_specs=[row] * (1 + n_norm),
        out_shape=[jax.ShapeDtypeStruct((m, d), F32)]
        + [jax.ShapeDtypeStruct((m, d), BF16)] * n_norm,
        scratch_shapes=[pltpu.VMEM((tm, d), F32)],
        compiler_params=_params(2),
        name="moe",
    )(x, comb, w_in, w_in, w_out, h, *[g.reshape(1, d) for g in gains])
    return outs[0], tuple(outs[1:])


def _moe_sparse_kernel(x_ref, c_ref, wg_ref, wu_ref, wo_ref, h_ref, *rest, n_norm, cap, half):
    g_refs = rest[:n_norm]
    ho_ref = rest[n_norm]
    xn_refs = rest[n_norm + 1:n_norm + 1 + n_norm]
    rankt_ref, combt_ref = rest[-2:]
    e = pl.program_id(1)
    n_half = x_ref.shape[0] // half

    @pl.when(e == 0)
    def _():
        ho_ref[...] = h_ref[...]
        r = lax.broadcasted_iota(jnp.int32, (half, half), 0)
        c = lax.broadcasted_iota(jnp.int32, (half, half), 1)
        before = jnp.where(c < r, 1.0, 0.0).astype(BF16)
        for hf in range(n_half):
            rows = pl.ds(hf * half, half)
            comb = c_ref[rows, :]
            sel = comb > 0.0
            rank = jnp.where(sel, _dot(before, jnp.where(sel, 1.0, 0.0).astype(BF16)), -1.0)
            rankt_ref[:, rows] = rank.T
            combt_ref[:, rows] = comb.T

    rk_row = rankt_ref[pl.ds(e, 1), :]
    gate_row = combt_ref[pl.ds(e, 1), :]
    n_routed = jnp.max(rk_row) + 1.0
    for ch in range(-(-half // cap)):
        @pl.when(n_routed > float(ch * cap))
        def _():
            i_col = (lax.broadcasted_iota(jnp.int32, (cap, 1), 0) + ch * cap).astype(F32)
            picks, xs, gcs = [], [], []
            for hf in range(n_half):
                cols = slice(hf * half, (hf + 1) * half)
                hit = rk_row[:, cols] == i_col
                pick = jnp.where(hit, 1.0, 0.0).astype(BF16)
                picks.append(pick)
                gcs.append(jnp.sum(jnp.where(hit, gate_row[:, cols], 0.0), axis=-1, keepdims=True))
                xs.append(_dot(pick, x_ref[pl.ds(hf * half, half), :]).astype(BF16))
            xs = jnp.concatenate(xs, axis=0)
            gate = _dot(xs, wg_ref[...])
            up = _dot(xs, wu_ref[...])
            act = (gate * jax.nn.sigmoid(gate) * up).astype(BF16)
            y = jnp.concatenate(gcs, axis=0) * _dot(act, wo_ref[...])
            hi = y.astype(BF16)
            lo = (y - hi.astype(F32)).astype(BF16)
            tn = (((0,), (0,)), ((), ()))
            for hf in range(n_half):
                sl = slice(hf * cap, (hf + 1) * cap)
                ho_ref[pl.ds(hf * half, half), :] += (
                    lax.dot_general(picks[hf], hi[sl], tn, preferred_element_type=F32)
                    + lax.dot_general(picks[hf], lo[sl], tn, preferred_element_type=F32))

    @pl.when(e == pl.num_programs(1) - 1)
    def _():
        if n_norm:
            h = ho_ref[...]
            y = h * lax.rsqrt(jnp.mean(h * h, axis=-1, keepdims=True) + EPS)
            for g_ref, xn_ref in zip(g_refs, xn_refs):
                xn_ref[...] = (y * g_ref[...]).astype(xn_ref.dtype)


def moe_sparse(x, comb, w_in, w_out, h, gains=(), tm=MOE_TILE, cap=MOE_CHUNK, half=MOE_HALF):
    m, d = x.shape
    n_exp, _, f2 = w_in.shape
    f = f2 // 2
    assert m % tm == 0 and tm % half == 0
    n_norm = len(gains)
    row = pl.BlockSpec((tm, d), lambda i, e: (i, 0))
    gspec = pl.BlockSpec((1, d), lambda i, e: (0, 0))
    outs = pl.pallas_call(
        functools.partial(_moe_sparse_kernel, n_norm=n_norm, cap=cap, half=half),
        grid=(m // tm, n_exp),
        in_specs=[row,
                  pl.BlockSpec((tm, LANES), lambda i, e: (i, 0)),
                  pl.BlockSpec((None, d, f), lambda i, e: (e, 0, 0)),
                  pl.BlockSpec((None, d, f), lambda i, e: (e, 0, 1)),
                  pl.BlockSpec((None, f, d), lambda i, e: (e, 0, 0)),
                  row] + [gspec] * n_norm,
        out_specs=[row] * (1 + n_norm),
        out_shape=[jax.ShapeDtypeStruct((m, d), F32)]
        + [jax.ShapeDtypeStruct((m, d), BF16)] * n_norm,
        scratch_shapes=[pltpu.VMEM((LANES, tm), F32), pltpu.VMEM((LANES, tm), F32)],
        compiler_params=_params(2),
        name="moe_sparse",
    )(x, comb, w_in, w_in, w_out, h, *[g.reshape(1, d) for g in gains])
    return outs[0], tuple(outs[1:])


def _softplus2(z):
    return jnp.maximum(jnp.log2(1.0 + jnp.exp2(jnp.minimum(z, 126.0))), z)


def _suffix_ones(n, keys_on_sublanes):
    r = lax.broadcasted_iota(jnp.int32, (n, n), 0)
    c = lax.broadcasted_iota(jnp.int32, (n, n), 1)
    return jnp.where((c >= r) if keys_on_sublanes else (r >= c), 1.0, 0.0).astype(BF16)


def _sb_prompt_kernel(q_ref, k_ref, vt_ref, o_ref, acc_ref, z_buf, d_buf, *, tq, bk, n_frame_blocks):
    t = pl.program_id(2)
    rows = SB_GROUP * tq
    qs = jnp.concatenate([q_ref[:, g * SB_DH:(g + 1) * SB_DH] for g in range(SB_GROUP)], axis=0)
    qt = qs.T
    qfrm = t * tq + lax.rem(lax.broadcasted_iota(jnp.int32, (1, rows), 1), tq)
    trit = _suffix_ones(bk, True)
    per_tile = tq // bk
    n_before = t * per_tile

    krow = lax.broadcasted_iota(jnp.int32, (bk, 1), 0)
    js = [n_before + per_tile - 1 - m for m in range(per_tile)]
    zs = [_dot(k_ref[j], qt) for j in js]
    valids = [(j * bk + krow) < qfrm for j in js]
    crun = jnp.zeros((1, rows), F32)
    ds = []
    for z, valid in zip(zs, valids):
        cum = _dot(trit, jnp.where(valid, _softplus2(z), 0.0).astype(BF16))
        ds.append(cum - crun)
        crun = crun - cum[0:1, :]
    acc = jnp.zeros((SB_DH, rows), F32)
    for j, z, valid, d in zip(js, zs, valids, ds):
        acc = acc + _dot(vt_ref[j], jnp.where(valid, jnp.exp2(z - d), 0.0).astype(BF16))
    acc_ref[...] = acc

    def blk(u):
        return jnp.where(u < n_before, n_before - 1 - u,
                         jnp.minimum(n_frame_blocks + u - n_before, n_frame_blocks + 1))

    def stage_z(u, slot):
        z_buf[slot] = _dot(k_ref[blk(u)], qt)

    def stage_d(slot, crun):
        cum = _dot(trit, _softplus2(z_buf[slot]).astype(BF16))
        d_buf[slot] = cum - crun
        return crun - cum[0:1, :]

    def stage_acc(u, slot):
        a = jnp.exp2(z_buf[slot] - d_buf[slot])
        acc_ref[...] += _dot(vt_ref[blk(u)], a.astype(BF16))

    stage_z(0, 0)
    stage_z(1, 1)
    crun = stage_d(0, crun)

    def steady(i, crun):
        for r in range(3):
            stage_z(3 * i + r + 2, (r + 2) % 3)
            crun = stage_d((r + 1) % 3, crun)
            stage_acc(3 * i + r, r)
        return crun

    lax.fori_loop(0, (n_before + 1 + 2) // 3, steady, crun)
    acc = acc_ref[...].T
    for g in range(SB_GROUP):
        o_ref[:, g * SB_DH:(g + 1) * SB_DH] = acc[g * tq:(g + 1) * tq].astype(o_ref.dtype)


def sb_attention_prompt(q, kb, vt, bk, tq_pref=512):
    b, lq, _ = q.shape
    nblk = kb.shape[2]
    tq = tq_pref if lq % tq_pref == 0 else bk
    assert lq % tq == 0 and tq % bk == 0 and nblk == lq // bk + 2
    gw = SB_GROUP * SB_DH
    rows = SB_GROUP * tq
    return pl.pallas_call(
        functools.partial(_sb_prompt_kernel, tq=tq, bk=bk, n_frame_blocks=lq // bk),
        grid=(b, SB_KV_HEADS, lq // tq),
        in_specs=[pl.BlockSpec((None, tq, gw), lambda bi, h, t: (bi, t, h)),
                  pl.BlockSpec((None, None, nblk, bk, SB_DH), lambda bi, h, t: (bi, h, 0, 0, 0)),
                  pl.BlockSpec((None, None, nblk, SB_DH, bk), lambda bi, h, t: (bi, h, 0, 0, 0))],
        out_specs=pl.BlockSpec((None, tq, gw), lambda bi, h, t: (bi, t, h)),
        out_shape=jax.ShapeDtypeStruct((b, lq, SB_HEADS * SB_DH), BF16),
        scratch_shapes=[pltpu.VMEM((SB_DH, rows), F32),
                        pltpu.VMEM((3, bk, rows), F32),
                        pltpu.VMEM((3, bk, rows), F32)],
        compiler_params=_params(3),
        name="sb_attention_prompt",
    )(q, kb, vt)


def _sb_sample_kernel(q_ref, kc_ref, vc_ref, kn_ref, vn_ref, o_ref, *, bk):
    ds = q_ref.shape[0]
    n_cache = kc_ref.shape[0]
    rows = SB_GROUP * ds
    qs = jnp.concatenate([q_ref[:, g * SB_DH:(g + 1) * SB_DH] for g in range(SB_GROUP)], axis=0)
    qrow = lax.rem(lax.broadcasted_iota(jnp.int32, (rows, 1), 0), ds)
    new_valid = lax.broadcasted_iota(jnp.int32, (1, ds), 1) < qrow

    spans = [(kn_ref, vn_ref, 0, ds, new_valid)]
    whole = n_cache // bk
    if n_cache % bk:
        spans.append((kc_ref, vc_ref, whole * bk, n_cache % bk, None))
    spans += [(kc_ref, vc_ref, j * bk, bk, None) for j in reversed(range(whole))]

    zs = [lax.dot_general(qs, kr[st:st + sz, :].astype(BF16), (((1,), (1,)), ((), ())),
                          preferred_element_type=F32) for kr, _, st, sz, _ in spans]
    tris = {sz: _suffix_ones(sz, False) for sz in {sp[3] for sp in spans}}
    crun = jnp.zeros((rows, 1), F32)
    dds = []
    for z, (_, _, _, sz, valid) in zip(zs, spans):
        sp = _softplus2(z)
        if valid is not None:
            sp = jnp.where(valid, sp, 0.0)
        cum = _dot(sp.astype(BF16), tris[sz])
        dds.append(cum - crun)
        crun = crun - cum[:, 0:1]
    acc = jnp.zeros((rows, SB_DH), F32)
    for z, d, (_, vr, st, sz, valid) in zip(zs, dds, spans):
        a = jnp.exp2(z - d)
        if valid is not None:
            a = jnp.where(valid, a, 0.0)
        acc = acc + _dot(a.astype(BF16), vr[st:st + sz, :].astype(BF16))
    for g in range(SB_GROUP):
        o_ref[:, g * SB_DH:(g + 1) * SB_DH] = acc[g * ds:(g + 1) * ds].astype(o_ref.dtype)


def sb_attention_sample(q, cache_k, cache_v, k_new, v_new, bk=256):
    b, ds, _ = q.shape
    n_cache = cache_k.shape[1]
    gw = SB_GROUP * SB_DH
    cache = pl.BlockSpec((None, n_cache, SB_DH), lambda bi, h: (bi, 0, h))
    new = pl.BlockSpec((None, ds, SB_DH), lambda bi, h: (bi, 0, h))
    qo = pl.BlockSpec((None, ds, gw), lambda bi, h: (bi, 0, h))
    return pl.pallas_call(
        functools.partial(_sb_sample_kernel, bk=bk),
        grid=(b, SB_KV_HEADS),
        in_specs=[qo, cache, cache, new, new],
        out_specs=qo,
        out_shape=jax.ShapeDtypeStruct((b, ds, SB_HEADS * SB_DH), BF16),
        compiler_params=_params(2),
        name="sb_attention_sample",
    )(q, cache_k, cache_v, k_new, v_new)


def kernel(x_prompt, x_sample, state_ret, cache_k, cache_v, meta_tokens, norm_mix, norm_ffn, w_ret_in, w_ret_out, norm_kv, w_kv, k_norm, w_sb_q, q_norm, w_sb_o, w_ffn_in, w_ffn_out, w_router, b_router, w_exp_in, w_exp_out):
    bp, seq, d = x_prompt.shape
    bs, ds, _ = x_sample.shape
    n_a = w_ret_in.shape[0]
    depth = norm_mix.shape[0]
    p0_s = cache_k.shape[1]
    m_main, m_s = bp * seq, bs * ds

    w_ret_in, w_ret_out, w_kv, w_sb_q, w_sb_o, w_ffn_in, w_ffn_out, w_exp_in, w_exp_out = (
        w.astype(BF16) for w in (w_ret_in, w_ret_out, w_kv, w_sb_q, w_sb_o, w_ffn_in, w_ffn_out,
                                 w_exp_in, w_exp_out))

    def channel(l, h, xn, gains):
        if l % 2 == 0:
            act = swiglu_in(xn, w_ffn_in[l // 2])
            return out_proj(act, w_ffn_out[l // 2], h, gains)
        comb = router(h, norm_ffn[l], w_router[l // 2], b_router[l // 2])
        experts = moe_sparse if h.shape[0] % MOE_TILE == 0 else moe
        return experts(xn, comb, w_exp_in[l // 2], w_exp_out[l // 2], h, gains)

    def next_gains(l):
        if l + 1 == depth:
            return ()
        if l + 1 == n_a:
            return (norm_kv, norm_mix[l + 1])
        return (norm_mix[l + 1],)

    h_s = jnp.concatenate([x_sample.reshape(m_s, d), meta_tokens.astype(x_prompt.dtype)], axis=0)
    h_m = x_prompt.reshape(m_main, d)
    xn_s = rmsnorm(h_s, norm_mix[0])
    xn_m = rmsnorm(h_m, norm_mix[0])
    pos_meta = jnp.arange(N_META)
    pos_s = p0_s + jnp.arange(ds)
    pos_m = N_META + jnp.arange(seq)
    zero_state = jnp.zeros((1, RET_HEADS, RET_DK, RET_DV), F32)

    ret_p, ret_s = [], []
    for l in range(n_a):
        p_s = matmul(xn_s, w_ret_in[l])
        o_s, st_s = retention(p_s, 0, bs, ds, ds, pos_s, state_ret[l].astype(F32), True)
        o_meta, st_meta = retention(p_s, m_s, 1, N_META, N_META, pos_meta, zero_state, False)
        p_m = matmul(xn_m, w_ret_in[l])
        o_m, st_m = retention(p_m, 0, bp, seq, RET_BLOCK, pos_m, st_meta, False)
        ret_s.append(st_s.astype(state_ret.dtype))
        ret_p.append(st_m.astype(x_prompt.dtype))
        h_s, (xn_s,) = out_proj(jnp.concatenate([o_s, o_meta], axis=0), w_ret_out[l], h_s,
                                (norm_ffn[l],))
        h_m, (xn_m,) = out_proj(o_m, w_ret_out[l], h_m, (norm_ffn[l],))
        h_s, xns_s = channel(l, h_s, xn_s, next_gains(l))
        h_m, xns_m = channel(l, h_m, xn_m, next_gains(l))
        if l + 1 < n_a:
            (xn_s,), (xn_m,) = xns_s, xns_m

    (xkv_s, xq_s), (xkv_m, xq_m) = xns_s, xns_m
    k_sm, v_sm = kv_proj(xkv_s, w_kv, k_norm)
    k_m, v_m, kb, vt = kv_proj_blocks(xkv_m, w_kv, k_norm, bp, SB_BLOCK, 2)
    nkv = SB_KV_HEADS * SB_DH
    k_meta, v_meta = k_sm[m_s:], v_sm[m_s:]

    def extra_blocks(meta_rows):
        blk = jnp.pad(meta_rows.astype(BF16), ((SB_BLOCK - N_META, 0), (0, 0)))
        blk = blk.reshape(SB_BLOCK, SB_KV_HEADS, SB_DH).transpose(1, 0, 2)
        both = jnp.stack([blk, jnp.zeros_like(blk)], axis=1)
        return jnp.broadcast_to(both[None], (bp,) + both.shape)

    n_fb = seq // SB_BLOCK
    kb = lax.dynamic_update_slice(kb, extra_blocks(k_meta), (0, 0, n_fb, 0, 0))
    vt = lax.dynamic_update_slice(vt, extra_blocks(v_meta).transpose(0, 1, 2, 4, 3), (0, 0, n_fb, 0, 0))

    def with_meta(meta_rows, frames):
        meta_b = jnp.broadcast_to(meta_rows[None], (bp, N_META, nkv))
        return jnp.concatenate([meta_b, frames.reshape(bp, seq, nkv)], axis=1)

    k_p = with_meta(k_meta, k_m)
    v_p = with_meta(v_meta, v_m)
    k_s = k_sm[:m_s].reshape(bs, ds, nkv)
    v_s = v_sm[:m_s].reshape(bs, ds, nkv)
    cache_k2 = cache_k.reshape(bs, p0_s, nkv)
    cache_v2 = cache_v.reshape(bs, p0_s, nkv)

    h_s, xn_s = h_s[:m_s], xq_s[:m_s]
    xn_m = xq_m
    for l in range(n_a, depth):
        j = l - n_a
        q_s = q_proj(xn_s, w_sb_q[j], q_norm[j]).reshape(bs, ds, d)
        q_m = q_proj(xn_m, w_sb_q[j], q_norm[j]).reshape(bp, seq, d)
        o_s = sb_attention_sample(q_s, cache_k2, cache_v2, k_s.astype(cache_k.dtype),
                                  v_s.astype(cache_v.dtype), SB_BLOCK).reshape(m_s, d)
        o_m = sb_attention_prompt(q_m, kb, vt, SB_BLOCK).reshape(m_main, d)
        h_s, (xn_s,) = out_proj(o_s, w_sb_o[j], h_s, (norm_ffn[l],))
        h_m, (xn_m,) = out_proj(o_m, w_sb_o[j], h_m, (norm_ffn[l],))
        h_s, xns_s = channel(l, h_s, xn_s, next_gains(l))
        h_m, xns_m = channel(l, h_m, xn_m, next_gains(l))
        if l + 1 < depth:
            (xn_s,), (xn_m,) = xns_s, xns_m

    y_prompt = h_m.reshape(bp, seq, d)
    y_sample = h_s.reshape(bs, ds, d)
    kv4 = lambda t: t.reshape(t.shape[0], t.shape[1], SB_KV_HEADS, SB_DH)
    return (y_prompt, y_sample, jnp.stack(ret_p), kv4(k_p), kv4(v_p), jnp.stack(ret_s),
            kv4(k_s), kv4(v_s))
```

```python
import functools

import jax
import jax.numpy as jnp
from jax import lax
from jax.experimental import pallas as pl
from jax.experimental.pallas import tpu as pltpu

D_MODEL = 1024
N_META = 16
RET_BLOCK = 256
SB_BLOCK = 256
MOE_TILE = 1024
MOE_HALF = 512
MOE_CHUNK = 160
RET_HEADS = 4
RET_DK = 256
RET_DV = 512
RET_IN = RET_HEADS * (2 * RET_DK + 2 * RET_DV)
RET_THETA_BASE = 10000.0
SB_HEADS = 8
SB_KV_HEADS = 4
SB_GROUP = SB_HEADS // SB_KV_HEADS
SB_DH = 128
N_EXP = 8
EPS = 1e-6
LOG2E = 1.4426950408889634

LANES = 128
SUBLANES = 8
SB_RUN = SB_BLOCK // SUBLANES
SB_CHAIN = 4
BF16_ROWS = 16
VMEM_LIMIT = 56 * 1024 * 1024

F32 = jnp.float32
BF16 = jnp.bfloat16


def _params(n_axes):
    return pltpu.CompilerParams(dimension_semantics=("arbitrary",) * n_axes,
                                vmem_limit_bytes=VMEM_LIMIT)


def _tile(n, pref, mult=BF16_ROWS):
    if n <= pref:
        return n
    t = (pref // mult) * mult
    while t >= mult:
        if n % t == 0:
            return t
        t -= mult
    return n


def _dot(a, b):
    return jnp.dot(a, b, preferred_element_type=F32)


def _rms(x, g):
    return (x * lax.rsqrt(jnp.mean(x * x, axis=-1, keepdims=True) + EPS)) * g


def _rmsnorm_kernel(x_ref, g_ref, o_ref):
    o_ref[...] = _rms(x_ref[...], g_ref[...]).astype(o_ref.dtype)


def rmsnorm(x, g, tm_pref=1024):
    m, d = x.shape
    tm = _tile(m, tm_pref)
    return pl.pallas_call(
        _rmsnorm_kernel,
        grid=(m // tm,),
        in_specs=[pl.BlockSpec((tm, d), lambda i: (i, 0)),
                  pl.BlockSpec((1, d), lambda i: (0, 0))],
        out_specs=pl.BlockSpec((tm, d), lambda i: (i, 0)),
        out_shape=jax.ShapeDtypeStruct((m, d), BF16),
        compiler_params=_params(1),
        name="rmsnorm",
    )(x, g.reshape(1, d))


def _mm_kernel(x_ref, w_ref, o_ref):
    o_ref[...] = _dot(x_ref[...], w_ref[...]).astype(o_ref.dtype)


def matmul(x, w, out_dtype=F32, tm_pref=1024, tn_pref=1024):
    m, k = x.shape
    n = w.shape[1]
    tm = _tile(m, tm_pref)
    tn = _tile(n, tn_pref, LANES)
    return pl.pallas_call(
        _mm_kernel,
        grid=(n // tn, m // tm),
        in_specs=[pl.BlockSpec((tm, k), lambda j, i: (i, 0)),
                  pl.BlockSpec((k, tn), lambda j, i: (0, j))],
        out_specs=pl.BlockSpec((tm, tn), lambda j, i: (i, j)),
        out_shape=jax.ShapeDtypeStruct((m, n), out_dtype),
        compiler_params=_params(2),
        name="matmul",
    )(x, w)


def _swiglu_kernel(x_ref, wg_ref, wu_ref, o_ref):
    x = x_ref[...]
    gate = _dot(x, wg_ref[...])
    up = _dot(x, wu_ref[...])
    o_ref[...] = (gate * jax.nn.sigmoid(gate) * up).astype(o_ref.dtype)


def swiglu_in(x, w_in, tm_pref=512, tn_pref=1408):
    m, k = x.shape
    f = w_in.shape[1] // 2
    tm = _tile(m, tm_pref)
    tn = _tile(f, tn_pref, LANES)
    nj = f // tn
    return pl.pallas_call(
        _swiglu_kernel,
        grid=(nj, m // tm),
        in_specs=[pl.BlockSpec((tm, k), lambda j, i: (i, 0)),
                  pl.BlockSpec((k, tn), lambda j, i: (0, j)),
                  pl.BlockSpec((k, tn), lambda j, i: (0, j + nj))],
        out_specs=pl.BlockSpec((tm, tn), lambda j, i: (i, j)),
        out_shape=jax.ShapeDtypeStruct((m, f), BF16),
        compiler_params=_params(2),
        name="swiglu_in",
    )(x, w_in, w_in)


def _out_kernel(x_ref, w_ref, h_ref, *rest, n_norm):
    g_refs = rest[:n_norm]
    ho_ref = rest[n_norm]
    xn_refs = rest[n_norm + 1:]
    h = h_ref[...] + _dot(x_ref[...], w_ref[...])
    ho_ref[...] = h
    if n_norm:
        y = h * lax.rsqrt(jnp.mean(h * h, axis=-1, keepdims=True) + EPS)
        for g_ref, xn_ref in zip(g_refs, xn_refs):
            xn_ref[...] = (y * g_ref[...]).astype(xn_ref.dtype)


def out_proj(x, w, h, gains=(), tm_pref=512):
    m, k = x.shape
    d = w.shape[1]
    tm = _tile(m, tm_pref)
    n_norm = len(gains)
    row = pl.BlockSpec((tm, d), lambda i: (i, 0))
    gspec = pl.BlockSpec((1, d), lambda i: (0, 0))
    outs = pl.pallas_call(
        functools.partial(_out_kernel, n_norm=n_norm),
        grid=(m // tm,),
        in_specs=[pl.BlockSpec((tm, k), lambda i: (i, 0)),
                  pl.BlockSpec((k, d), lambda i: (0, 0)),
                  row] + [gspec] * n_norm,
        out_specs=[row] * (1 + n_norm),
        out_shape=[jax.ShapeDtypeStruct((m, d), F32)]
        + [jax.ShapeDtypeStruct((m, d), BF16)] * n_norm,
        compiler_params=_params(1),
        name="out_proj",
    )(x, w, h, *[g.reshape(1, d) for g in gains])
    return outs[0], tuple(outs[1:])


def _head_norm(y, g, n_heads):
    parts = []
    for hd in range(n_heads):
        yh = y[:, hd * SB_DH:(hd + 1) * SB_DH]
        parts.append(_rms(yh, g))
    return parts


def _qproj_kernel(x_ref, w_ref, g_ref, o_ref):
    y = _dot(x_ref[...], w_ref[...])
    g = g_ref[...]
    for hd, yh in enumerate(_head_norm(y, g, SB_HEADS)):
        o_ref[:, hd * SB_DH:(hd + 1) * SB_DH] = (yh * (SB_DH ** -0.5 * LOG2E)).astype(o_ref.dtype)


def q_proj(x, w, g, tm_pref=512):
    m, k = x.shape
    n = w.shape[1]
    tm = _tile(m, tm_pref)
    return pl.pallas_call(
        _qproj_kernel,
        grid=(m // tm,),
        in_specs=[pl.BlockSpec((tm, k), lambda i: (i, 0)),
                  pl.BlockSpec((k, n), lambda i: (0, 0)),
                  pl.BlockSpec((1, SB_DH), lambda i: (0, 0))],
        out_specs=pl.BlockSpec((tm, n), lambda i: (i, 0)),
        out_shape=jax.ShapeDtypeStruct((m, n), BF16),
        compiler_params=_params(1),
        name="q_proj",
    )(x, w, g.reshape(1, SB_DH))


def _kvproj_kernel(x_ref, w_ref, g_ref, k_ref, v_ref):
    y = _dot(x_ref[...], w_ref[...])
    nk = SB_KV_HEADS * SB_DH
    g = g_ref[...]
    for hd, yh in enumerate(_head_norm(y[:, :nk], g, SB_KV_HEADS)):
        k_ref[:, hd * SB_DH:(hd + 1) * SB_DH] = yh
    v_ref[...] = y[:, nk:]


def kv_proj(x, w, g, tm_pref=512):
    m, k = x.shape
    n = w.shape[1]
    nk = n // 2
    tm = _tile(m, tm_pref)
    return pl.pallas_call(
        _kvproj_kernel,
        grid=(m // tm,),
        in_specs=[pl.BlockSpec((tm, k), lambda i: (i, 0)),
                  pl.BlockSpec((k, n), lambda i: (0, 0)),
                  pl.BlockSpec((1, SB_DH), lambda i: (0, 0))],
        out_specs=[pl.BlockSpec((tm, nk), lambda i: (i, 0))] * 2,
        out_shape=[jax.ShapeDtypeStruct((m, nk), F32)] * 2,
        compiler_params=_params(1),
        name="kv_proj",
    )(x, w, g.reshape(1, SB_DH))


def _kvproj_blocks_kernel(x_ref, w_ref, g_ref, k_ref, v_ref, kb_ref, vt_ref, *, bk):
    y = _dot(x_ref[...], w_ref[...])
    nk = SB_KV_HEADS * SB_DH
    n_sub = x_ref.shape[0] // bk
    g = g_ref[...]
    for hd, yh in enumerate(_head_norm(y[:, :nk], g, SB_KV_HEADS)):
        k_ref[:, hd * SB_DH:(hd + 1) * SB_DH] = yh
        for r in range(n_sub):
            kb_ref[hd, r] = yh[r * bk:(r + 1) * bk].astype(BF16)
    v_ref[...] = y[:, nk:]
    for hd in range(SB_KV_HEADS):
        vh = y[:, nk + hd * SB_DH:nk + (hd + 1) * SB_DH]
        for r in range(n_sub):
            vt_ref[hd, r] = vh[r * bk:(r + 1) * bk].T.astype(BF16)


def kv_proj_blocks(x, w, g, batch, bk, n_extra, tm_pref=512):
    m, k = x.shape
    n = w.shape[1]
    nk = n // 2
    length = m // batch
    tm = tm_pref if length % tm_pref == 0 else bk
    assert length % tm == 0 and tm % bk == 0
    steps, n_sub = length // tm, tm // bk
    nblk = length // bk + n_extra
    return pl.pallas_call(
        functools.partial(_kvproj_blocks_kernel, bk=bk),
        grid=(batch, steps),
        in_specs=[pl.BlockSpec((tm, k), lambda b, i: (b * steps + i, 0)),
                  pl.BlockSpec((k, n), lambda b, i: (0, 0)),
                  pl.BlockSpec((1, SB_DH), lambda b, i: (0, 0))],
        out_specs=[pl.BlockSpec((tm, nk), lambda b, i: (b * steps + i, 0))] * 2
        + [pl.BlockSpec((None, SB_KV_HEADS, n_sub, bk, SB_DH), lambda b, i: (b, 0, i, 0, 0)),
           pl.BlockSpec((None, SB_KV_HEADS, n_sub, SB_DH, bk), lambda b, i: (b, 0, i, 0, 0))],
        out_shape=[jax.ShapeDtypeStruct((m, nk), F32)] * 2
        + [jax.ShapeDtypeStruct((batch, SB_KV_HEADS, nblk, bk, SB_DH), BF16),
           jax.ShapeDtypeStruct((batch, SB_KV_HEADS, nblk, SB_DH, bk), BF16)],
        compiler_params=_params(2),
        name="kv_proj_blocks",
    )(x, w, g.reshape(1, SB_DH))


def _rope(x, cos, sin):
    half = RET_DK // 2
    x1, x2 = x[:, :half], x[:, half:]
    return jnp.concatenate([x1 * cos - x2 * sin, x2 * cos + x1 * sin], axis=1)


def _ret_kernel(pq_ref, pk_ref, pv_ref, pg_ref, cos_ref, sin_ref, dmat_ref, qdec_ref, kdec_ref,
                sdec_ref, s0_ref, o_ref, sout_ref, s_scr, *, chunk, n_sub):
    step = pl.program_id(2)

    @pl.when(step == 0)
    def _():
        s_scr[...] = s0_ref[...]

    dmat = dmat_ref[...]
    qdec = qdec_ref[...]
    kdec = kdec_ref[...]
    sdec = sdec_ref[...]
    for c in range(n_sub):
        rows = pl.ds(c * chunk, chunk)
        cos = cos_ref[rows, :]
        sin = sin_ref[rows, :]
        q = _rope(pq_ref[rows, :], cos, sin)
        k = _rope(pk_ref[rows, :], cos, sin) * (RET_DK ** -0.5)
        v = pv_ref[rows, :].astype(BF16)
        s = s_scr[...]
        scores = lax.dot_general(q.astype(BF16), k.astype(BF16), (((1,), (1,)), ((), ())),
                                 preferred_element_type=F32) * dmat
        o = _dot(scores.astype(BF16), v) + _dot((q * qdec).astype(BF16), s.astype(BF16))
        kd_t = (k * kdec).T.astype(BF16)
        s_scr[...] = sdec * s + _dot(kd_t, v)
        o = o * lax.rsqrt(jnp.mean(o * o, axis=-1, keepdims=True) + EPS)
        g = pg_ref[rows, :]
        o_ref[rows, :] = (g * jax.nn.sigmoid(g) * o).astype(o_ref.dtype)

    @pl.when(step == pl.num_programs(2) - 1)
    def _():
        sout_ref[...] = s_scr[...]


def _ret_tables(chunk):
    lg = jnp.log(1.0 - 2.0 ** (-5.0 - jnp.arange(RET_HEADS, dtype=F32)))[:, None]
    i = jnp.arange(chunk, dtype=F32)
    diff = i[:, None] - i[None, :]
    dmat = jnp.where(diff >= 0, jnp.exp(jnp.maximum(diff, 0.0)[None] * lg[:, :, None]), 0.0)
    q_dec = jnp.exp((i + 1.0)[None] * lg)
    k_dec = jnp.exp((chunk - 1.0 - i)[None] * lg)
    s_dec = jnp.exp(chunk * lg)
    qdec = jnp.broadcast_to(q_dec[:, :, None], (RET_HEADS, chunk, RET_DK))
    kdec = jnp.broadcast_to(k_dec[:, :, None], (RET_HEADS, chunk, RET_DK))
    sdec = jnp.broadcast_to(s_dec[:, :, None], (RET_HEADS, 1, RET_DV))
    return dmat, qdec, kdec, sdec


def _rope_tables(pos):
    half = RET_DK // 2
    inv_freq = 1.0 / (RET_THETA_BASE ** jnp.linspace(0.0, 1.0, half, dtype=F32))
    ang = pos.astype(F32)[:, None] * inv_freq[None, :]
    return jnp.cos(ang), jnp.sin(ang)


def retention(p, row0, n_streams, length, chunk, pos, s0, per_stream_s0, rows_pref=512):
    n_sub = max(1, min(rows_pref // chunk, length // chunk))
    while (length // chunk) % n_sub:
        n_sub -= 1
    rows = n_sub * chunk
    steps = length // rows
    assert row0 % rows == 0
    base = row0 // rows
    cos, sin = _rope_tables(pos)
    dmat, qdec, kdec, sdec = _ret_tables(chunk)
    qb, vb = RET_HEADS, (2 * RET_HEADS * RET_DK) // RET_DV

    def prow(b, i):
        return base + b * steps + i

    s0_map = (lambda b, h, i: (b, h, 0, 0)) if per_stream_s0 else (lambda b, h, i: (0, h, 0, 0))
    return pl.pallas_call(
        functools.partial(_ret_kernel, chunk=chunk, n_sub=n_sub),
        grid=(n_streams, RET_HEADS, steps),
        in_specs=[
            pl.BlockSpec((rows, RET_DK), lambda b, h, i: (prow(b, i), h)),
            pl.BlockSpec((rows, RET_DK), lambda b, h, i: (prow(b, i), qb + h)),
            pl.BlockSpec((rows, RET_DV), lambda b, h, i: (prow(b, i), vb + h)),
            pl.BlockSpec((rows, RET_DV), lambda b, h, i: (prow(b, i), vb + RET_HEADS + h)),
            pl.BlockSpec((rows, RET_DK // 2), lambda b, h, i: (i, 0)),
            pl.BlockSpec((rows, RET_DK // 2), lambda b, h, i: (i, 0)),
            pl.BlockSpec((None, chunk, chunk), lambda b, h, i: (h, 0, 0)),
            pl.BlockSpec((None, chunk, RET_DK), lambda b, h, i: (h, 0, 0)),
            pl.BlockSpec((None, chunk, RET_DK), lambda b, h, i: (h, 0, 0)),
            pl.BlockSpec((None, 1, RET_DV), lambda b, h, i: (h, 0, 0)),
            pl.BlockSpec((None, None, RET_DK, RET_DV), s0_map),
        ],
        out_specs=[
            pl.BlockSpec((rows, RET_DV), lambda b, h, i: (b * steps + i, h)),
            pl.BlockSpec((None, None, RET_DK, RET_DV), lambda b, h, i: (b, h, 0, 0)),
        ],
        out_shape=[
            jax.ShapeDtypeStruct((n_streams * length, RET_HEADS * RET_DV), BF16),
            jax.ShapeDtypeStruct((n_streams, RET_HEADS, RET_DK, RET_DV), F32),
        ],
        scratch_shapes=[pltpu.VMEM((RET_DK, RET_DV), F32)],
        compiler_params=_params(3),
        name="retention",
    )(p, p, p, p, cos, sin, dmat, qdec, kdec, sdec, s0)


def _router_kernel(h_ref, g_ref, w_ref, b_ref, c_ref):
    xn = _rms(h_ref[...], g_ref[...])
    logits = jnp.dot(xn, w_ref[...], precision=lax.Precision.HIGHEST,
                     preferred_element_type=F32) + b_ref[...]
    lane = lax.broadcasted_iota(jnp.int32, logits.shape, 1).astype(F32)
    neg = jnp.float32(-jnp.inf)
    logits = jnp.where(lane < N_EXP, logits, neg)
    m1 = jnp.max(logits, axis=-1, keepdims=True)
    i1 = jnp.min(jnp.where(logits == m1, lane, float(LANES)), axis=-1, keepdims=True)
    rest = jnp.where(lane == i1, neg, logits)
    m2 = jnp.max(rest, axis=-1, keepdims=True)
    i2 = jnp.min(jnp.where(rest == m2, lane, float(LANES)), axis=-1, keepdims=True)
    e2 = jnp.exp(m2 - m1)
    den = 1.0 + e2
    c_ref[...] = jnp.where(lane == i1, 1.0 / den, 0.0) + jnp.where(lane == i2, e2 / den, 0.0)


def router(h, g, w_r, b_r, tm_pref=1024):
    m, d = h.shape
    tm = _tile(m, tm_pref)
    w_pad = jnp.zeros((d, LANES), F32).at[:, :N_EXP].set(w_r.astype(F32))
    b_pad = jnp.zeros((1, LANES), F32).at[0, :N_EXP].set(b_r.astype(F32))
    return pl.pallas_call(
        _router_kernel,
        grid=(m // tm,),
        in_specs=[pl.BlockSpec((tm, d), lambda i: (i, 0)),
                  pl.BlockSpec((1, d), lambda i: (0, 0)),
                  pl.BlockSpec((d, LANES), lambda i: (0, 0)),
                  pl.BlockSpec((1, LANES), lambda i: (0, 0))],
        out_specs=pl.BlockSpec((tm, LANES), lambda i: (i, 0)),
        out_shape=jax.ShapeDtypeStruct((m, LANES), F32),
        compiler_params=_params(1),
        name="router",
    )(h, g.reshape(1, d), w_pad, b_pad)


def _moe_kernel(x_ref, c_ref, wg_ref, wu_ref, wo_ref, h_ref, *rest, n_norm):
    g_refs = rest[:n_norm]
    ho_ref = rest[n_norm]
    xn_refs = rest[n_norm + 1:n_norm + 1 + n_norm]
    acc = rest[-1]
    e = pl.program_id(1)

    @pl.when(e == 0)
    def _():
        acc[...] = h_ref[...]

    comb = c_ref[...]
    lane = lax.broadcasted_iota(jnp.int32, comb.shape, 1)
    ce = jnp.sum(jnp.where(lane == e, comb, 0.0), axis=-1, keepdims=True)

    @pl.when(jnp.max(ce) > 0.0)
    def _():
        x = x_ref[...]
        gate = _dot(x, wg_ref[...])
        up = _dot(x, wu_ref[...])
        act = (gate * jax.nn.sigmoid(gate) * up).astype(BF16)
        acc[...] += ce * _dot(act, wo_ref[...])

    @pl.when(e == pl.num_programs(1) - 1)
    def _():
        h = acc[...]
        ho_ref[...] = h
        if n_norm:
            y = h * lax.rsqrt(jnp.mean(h * h, axis=-1, keepdims=True) + EPS)
            for g_ref, xn_ref in zip(g_refs, xn_refs):
                xn_ref[...] = (y * g_ref[...]).astype(xn_ref.dtype)


def moe(x, comb, w_in, w_out, h, gains=(), tm_pref=512):
    m, d = x.shape
    n_exp, _, f2 = w_in.shape
    f = f2 // 2
    tm = _tile(m, tm_pref)
    n_norm = len(gains)
    row = pl.BlockSpec((tm, d), lambda i, e: (i, 0))
    gspec = pl.BlockSpec((1, d), lambda i, e: (0, 0))
    outs = pl.pallas_call(
        functools.partial(_moe_kernel, n_norm=n_norm),
        grid=(m // tm, n_exp),
        in_specs=[row,
                  pl.BlockSpec((tm, LANES), lambda i, e: (i, 0)),
                  pl.BlockSpec((None, d, f), lambda i, e: (e, 0, 0)),
                  pl.BlockSpec((None, d, f), lambda i, e: (e, 0, 1)),
                  pl.BlockSpec((None, f, d), lambda i, e: (e, 0, 0)),
                  row] + [gspec] * n_norm,
        out_specs=[row] * (1 + n_norm),
        out_shape=[jax.ShapeDtypeStruct((m, d), F32)]
        + [jax.ShapeDtypeStruct((m, d), BF16)] * n_norm,
        scratch_shapes=[pltpu.VMEM((tm, d), F32)],
        compiler_params=_params(2),
        name="moe",
    )(x, comb, w_in, w_in, w_out, h, *[g.reshape(1, d) for g in gains])
    return outs[0], tuple(outs[1:])


def _moe_sparse_kernel(x_ref, c_ref, wg_ref, wu_ref, wo_ref, h_ref, *rest, n_norm, cap, half):
    g_refs = rest[:n_norm]
    ho_ref = rest[n_norm]
    xn_refs = rest[n_norm + 1:n_norm + 1 + n_norm]
    rankt_ref, combt_ref = rest[-2:]
    e = pl.program_id(1)
    n_half = x_ref.shape[0] // half

    @pl.when(e == 0)
    def _():
        ho_ref[...] = h_ref[...]
        r = lax.broadcasted_iota(jnp.int32, (half, half), 0)
        c = lax.broadcasted_iota(jnp.int32, (half, half), 1)
        before = jnp.where(c < r, 1.0, 0.0).astype(BF16)
        for hf in range(n_half):
            rows = pl.ds(hf * half, half)
            comb = c_ref[rows, :]
            sel = comb > 0.0
            rank = jnp.where(sel, _dot(before, jnp.where(sel, 1.0, 0.0).astype(BF16)), -1.0)
            rankt_ref[:, rows] = rank.T
            combt_ref[:, rows] = comb.T

    rk_row = rankt_ref[pl.ds(e, 1), :]
    gate_row = combt_ref[pl.ds(e, 1), :]
    n_routed = jnp.max(rk_row) + 1.0
    for ch in range(-(-half // cap)):
        @pl.when(n_routed > float(ch * cap))
        def _():
            i_col = (lax.broadcasted_iota(jnp.int32, (cap, 1), 0) + ch * cap).astype(F32)
            picks, xs, gcs = [], [], []
            for hf in range(n_half):
                cols = slice(hf * half, (hf + 1) * half)
                hit = rk_row[:, cols] == i_col
                pick = jnp.where(hit, 1.0, 0.0).astype(BF16)
                picks.append(pick)
                gcs.append(jnp.sum(jnp.where(hit, gate_row[:, cols], 0.0), axis=-1, keepdims=True))
                xs.append(_dot(pick, x_ref[pl.ds(hf * half, half), :]).astype(BF16))
            xs = jnp.concatenate(xs, axis=0)
            gate = _dot(xs, wg_ref[...])
            up = _dot(xs, wu_ref[...])
            act = (gate * jax.nn.sigmoid(gate) * up).astype(BF16)
            y = jnp.concatenate(gcs, axis=0) * _dot(act, wo_ref[...])
            hi = y.astype(BF16)
            lo = (y - hi.astype(F32)).astype(BF16)
            tn = (((0,), (0,)), ((), ()))
            for hf in range(n_half):
                sl = slice(hf * cap, (hf + 1) * cap)
                ho_ref[pl.ds(hf * half, half), :] += (
                    lax.dot_general(picks[hf], hi[sl], tn, preferred_element_type=F32)
                    + lax.dot_general(picks[hf], lo[sl], tn, preferred_element_type=F32))

    @pl.when(e == pl.num_programs(1) - 1)
    def _():
        if n_norm:
            h = ho_ref[...]
            y = h * lax.rsqrt(jnp.mean(h * h, axis=-1, keepdims=True) + EPS)
            for g_ref, xn_ref in zip(g_refs, xn_refs):
                xn_ref[...] = (y * g_ref[...]).astype(xn_ref.dtype)


def moe_sparse(x, comb, w_in, w_out, h, gains=(), tm=MOE_TILE, cap=MOE_CHUNK, half=MOE_HALF):
    m, d = x.shape
    n_exp, _, f2 = w_in.shape
    f = f2 // 2
    assert m % tm == 0 and tm % half == 0
    n_norm = len(gains)
    row = pl.BlockSpec((tm, d), lambda i, e: (i, 0))
    gspec = pl.BlockSpec((1, d), lambda i, e: (0, 0))
    outs = pl.pallas_call(
        functools.partial(_moe_sparse_kernel, n_norm=n_norm, cap=cap, half=half),
        grid=(m // tm, n_exp),
        in_specs=[row,
                  pl.BlockSpec((tm, LANES), lambda i, e: (i, 0)),
                  pl.BlockSpec((None, d, f), lambda i, e: (e, 0, 0)),
                  pl.BlockSpec((None, d, f), lambda i, e: (e, 0, 1)),
                  pl.BlockSpec((None, f, d), lambda i, e: (e, 0, 0)),
                  row] + [gspec] * n_norm,
        out_specs=[row] * (1 + n_norm),
        out_shape=[jax.ShapeDtypeStruct((m, d), F32)]
        + [jax.ShapeDtypeStruct((m, d), BF16)] * n_norm,
        scratch_shapes=[pltpu.VMEM((LANES, tm), F32), pltpu.VMEM((LANES, tm), F32)],
        compiler_params=_params(2),
        name="moe_sparse",
    )(x, comb, w_in, w_in, w_out, h, *[g.reshape(1, d) for g in gains])
    return outs[0], tuple(outs[1:])


def _softplus2(z):
    return jnp.maximum(jnp.log2(1.0 + jnp.exp2(jnp.minimum(z, 126.0))), z)


def _suffix_ones(n, keys_on_sublanes):
    r = lax.broadcasted_iota(jnp.int32, (n, n), 0)
    c = lax.broadcasted_iota(jnp.int32, (n, n), 1)
    return jnp.where((c >= r) if keys_on_sublanes else (r >= c), 1.0, 0.0).astype(BF16)


def _shift_up(x, k, sl):
    return jnp.where(sl < SUBLANES - k, pltpu.roll(x, SUBLANES - k, axis=0), 1.0)


def _sb_weights(z_ref, pre_ref, a_ref, slot, carry, valid_fn=None):
    rows = z_ref.shape[2]
    n_chain = SB_RUN // SB_CHAIN
    totals = []
    for c in range(n_chain):
        sub = jnp.ones((SUBLANES, rows), F32)
        for v in reversed(range(c * SB_CHAIN, (c + 1) * SB_CHAIN)):
            band = pl.ds(SUBLANES * v, SUBLANES)
            u = jnp.exp2(jnp.minimum(z_ref[slot, band, :], 126.0))
            miss = 1.0 / (1.0 + u)
            hit = u * miss
            if valid_fn is not None:
                ok = valid_fn(v)
                miss = jnp.where(ok, miss, 1.0)
                hit = jnp.where(ok, hit, 0.0)
            pre_ref[slot, band, :] = hit * sub
            sub = sub * miss
        totals.append(sub)
    past = [None] * n_chain
    run = jnp.ones((SUBLANES, rows), F32)
    for c in reversed(range(n_chain)):
        past[c] = run
        run = run * totals[c]
    sl = lax.broadcasted_iota(jnp.int32, (SUBLANES, 1), 0)
    later = _shift_up(run, 1, sl)
    later = later * _shift_up(later, 1, sl)
    later = later * _shift_up(later, 2, sl)
    later = later * _shift_up(later, 4, sl)
    later = later * carry
    for c in range(n_chain):
        f = past[c] * later
        f2 = jnp.concatenate([f, f], axis=0)
        for v in range(c * SB_CHAIN, (c + 1) * SB_CHAIN, 2):
            band2 = pl.ds(SUBLANES * v, 2 * SUBLANES)
            a_ref[slot, band2, :] = (pre_ref[slot, band2, :] * f2).astype(BF16)
    return (later * run)[0:1, :]


def _sb_prompt_kernel(q_ref, k_ref, vt_ref, o_ref, acc_ref, z_buf, pre_buf, a_buf, *, tq, bk, n_frame_blocks):
    t = pl.program_id(2)
    rows = SB_GROUP * tq
    qs = jnp.concatenate([q_ref[:, g * SB_DH:(g + 1) * SB_DH] for g in range(SB_GROUP)], axis=0)
    qt = qs.T
    qfrm = t * tq + lax.rem(lax.broadcasted_iota(jnp.int32, (1, rows), 1), tq)
    per_tile = tq // bk
    n_before = t * per_tile
    sl = lax.broadcasted_iota(jnp.int32, (SUBLANES, 1), 0)

    def blk(u):
        return jnp.where(u < n_before, n_before - 1 - u,
                         jnp.minimum(n_frame_blocks + u - n_before, n_frame_blocks + 1))

    def stage_z(j, slot):
        z_buf[slot] = _dot(k_ref[j], qt)

    def stage_acc(j, slot):
        acc_ref[...] += _dot(vt_ref[j], a_buf[slot])

    assert per_tile < 3
    carry = jnp.ones((1, rows), F32)
    acc_ref[...] = jnp.zeros_like(acc_ref)
    diag = [n_before + per_tile - 1 - m for m in range(per_tile)]
    for m, j in enumerate(diag):
        stage_z(j, m)
    stage_z(blk(0), per_tile)
    for m, j in enumerate(diag):
        carry = _sb_weights(z_buf, pre_buf, a_buf, m, carry,
                            lambda v, j=j: (j * bk + SB_RUN * sl + v) < qfrm)
    for m, j in enumerate(diag):
        stage_acc(j, m)
    stage_z(blk(1), (per_tile + 1) % 3)
    carry = _sb_weights(z_buf, pre_buf, a_buf, per_tile, carry)

    def steady(i, carry):
        for r in range(3):
            stage_z(blk(3 * i + r + 2), (per_tile + r + 2) % 3)
            carry = _sb_weights(z_buf, pre_buf, a_buf, (per_tile + r + 1) % 3, carry)
            stage_acc(blk(3 * i + r), (per_tile + r) % 3)
        return carry

    lax.fori_loop(0, (n_before + 1 + 2) // 3, steady, carry)
    acc = acc_ref[...].T
    for g in range(SB_GROUP):
        o_ref[:, g * SB_DH:(g + 1) * SB_DH] = acc[g * tq:(g + 1) * tq].astype(o_ref.dtype)


def sb_attention_prompt(q, kb, vt, bk, tq_pref=256):
    b, lq, _ = q.shape
    nblk = kb.shape[2]
    tq = tq_pref if lq % tq_pref == 0 else bk
    assert lq % tq == 0 and tq % bk == 0 and nblk == lq // bk + 2 and bk == SUBLANES * SB_RUN
    kb = kb.reshape(b, SB_KV_HEADS, nblk, SUBLANES, SB_RUN, SB_DH).swapaxes(3, 4).reshape(kb.shape)
    vt = vt.reshape(b, SB_KV_HEADS, nblk, SB_DH, SUBLANES, SB_RUN).swapaxes(4, 5).reshape(vt.shape)
    gw = SB_GROUP * SB_DH
    rows = SB_GROUP * tq
    return pl.pallas_call(
        functools.partial(_sb_prompt_kernel, tq=tq, bk=bk, n_frame_blocks=lq // bk),
        grid=(b, SB_KV_HEADS, lq // tq),
        in_specs=[pl.BlockSpec((None, tq, gw), lambda bi, h, t: (bi, t, h)),
                  pl.BlockSpec((None, None, nblk, bk, SB_DH), lambda bi, h, t: (bi, h, 0, 0, 0)),
                  pl.BlockSpec((None, None, nblk, SB_DH, bk), lambda bi, h, t: (bi, h, 0, 0, 0))],
        out_specs=pl.BlockSpec((None, tq, gw), lambda bi, h, t: (bi, t, h)),
        out_shape=jax.ShapeDtypeStruct((b, lq, SB_HEADS * SB_DH), BF16),
        scratch_shapes=[pltpu.VMEM((SB_DH, rows), F32),
                        pltpu.VMEM((3, bk, rows), F32),
                        pltpu.VMEM((3, bk, rows), F32),
                        pltpu.VMEM((3, bk, rows), BF16)],
        compiler_params=_params(3),
        name="sb_attention_prompt",
    )(q, kb, vt)


def _sb_sample_kernel(q_ref, kc_ref, vc_ref, kn_ref, vn_ref, o_ref, *, bk):
    ds = q_ref.shape[0]
    n_cache = kc_ref.shape[0]
    rows = SB_GROUP * ds
    qs = jnp.concatenate([q_ref[:, g * SB_DH:(g + 1) * SB_DH] for g in range(SB_GROUP)], axis=0)
    qrow = lax.rem(lax.broadcasted_iota(jnp.int32, (rows, 1), 0), ds)
    new_valid = lax.broadcasted_iota(jnp.int32, (1, ds), 1) < qrow

    spans = [(kn_ref, vn_ref, 0, ds, new_valid)]
    whole = n_cache // bk
    if n_cache % bk:
        spans.append((kc_ref, vc_ref, whole * bk, n_cache % bk, None))
    spans += [(kc_ref, vc_ref, j * bk, bk, None) for j in reversed(range(whole))]

    zs = [lax.dot_general(qs, kr[st:st + sz, :].astype(BF16), (((1,), (1,)), ((), ())),
                          preferred_element_type=F32) for kr, _, st, sz, _ in spans]
    tris = {sz: _suffix_ones(sz, False) for sz in {sp[3] for sp in spans}}
    crun = jnp.zeros((rows, 1), F32)
    dds = []
    for z, (_, _, _, sz, valid) in zip(zs, spans):
        sp = _softplus2(z)
        if valid is not None:
            sp = jnp.where(valid, sp, 0.0)
        cum = _dot(sp.astype(BF16), tris[sz])
        dds.append(cum - crun)
        crun = crun - cum[:, 0:1]
    acc = jnp.zeros((rows, SB_DH), F32)
    for z, d, (_, vr, st, sz, valid) in zip(zs, dds, spans):
        a = jnp.exp2(z - d)
        if valid is not None:
            a = jnp.where(valid, a, 0.0)
        acc = acc + _dot(a.astype(BF16), vr[st:st + sz, :].astype(BF16))
    for g in range(SB_GROUP):
        o_ref[:, g * SB_DH:(g + 1) * SB_DH] = acc[g * ds:(g + 1) * ds].astype(o_ref.dtype)


def sb_attention_sample(q, cache_k, cache_v, k_new, v_new, bk=256):
    b, ds, _ = q.shape
    n_cache = cache_k.shape[1]
    gw = SB_GROUP * SB_DH
    cache = pl.BlockSpec((None, n_cache, SB_DH), lambda bi, h: (bi, 0, h))
    new = pl.BlockSpec((None, ds, SB_DH), lambda bi, h: (bi, 0, h))
    qo = pl.BlockSpec((None, ds, gw), lambda bi, h: (bi, 0, h))
    return pl.pallas_call(
        functools.partial(_sb_sample_kernel, bk=bk),
        grid=(b, SB_KV_HEADS),
        in_specs=[qo, cache, cache, new, new],
        out_specs=qo,
        out_shape=jax.ShapeDtypeStruct((b, ds, SB_HEADS * SB_DH), BF16),
        compiler_params=_params(2),
        name="sb_attention_sample",
    )(q, cache_k, cache_v, k_new, v_new)


def kernel(x_prompt, x_sample, state_ret, cache_k, cache_v, meta_tokens, norm_mix, norm_ffn, w_ret_in, w_ret_out, norm_kv, w_kv, k_norm, w_sb_q, q_norm, w_sb_o, w_ffn_in, w_ffn_out, w_router, b_router, w_exp_in, w_exp_out):
    bp, seq, d = x_prompt.shape
    bs, ds, _ = x_sample.shape
    n_a = w_ret_in.shape[0]
    depth = norm_mix.shape[0]
    p0_s = cache_k.shape[1]
    m_main, m_s = bp * seq, bs * ds

    w_ret_in, w_ret_out, w_kv, w_sb_q, w_sb_o, w_ffn_in, w_ffn_out, w_exp_in, w_exp_out = (
        w.astype(BF16) for w in (w_ret_in, w_ret_out, w_kv, w_sb_q, w_sb_o, w_ffn_in, w_ffn_out,
                                 w_exp_in, w_exp_out))

    def channel(l, h, xn, gains):
        if l % 2 == 0:
            act = swiglu_in(xn, w_ffn_in[l // 2])
            return out_proj(act, w_ffn_out[l // 2], h, gains)
        comb = router(h, norm_ffn[l], w_router[l // 2], b_router[l // 2])
        experts = moe_sparse if h.shape[0] % MOE_TILE == 0 else moe
        return experts(xn, comb, w_exp_in[l // 2], w_exp_out[l // 2], h, gains)

    def next_gains(l):
        if l + 1 == depth:
            return ()
        if l + 1 == n_a:
            return (norm_kv, norm_mix[l + 1])
        return (norm_mix[l + 1],)

    h_s = jnp.concatenate([x_sample.reshape(m_s, d), meta_tokens.astype(x_prompt.dtype)], axis=0)
    h_m = x_prompt.reshape(m_main, d)
    xn_s = rmsnorm(h_s, norm_mix[0])
    xn_m = rmsnorm(h_m, norm_mix[0])
    pos_meta = jnp.arange(N_META)
    pos_s = p0_s + jnp.arange(ds)
    pos_m = N_META + jnp.arange(seq)
    zero_state = jnp.zeros((1, RET_HEADS, RET_DK, RET_DV), F32)

    ret_p, ret_s = [], []
    for l in range(n_a):
        p_s = matmul(xn_s, w_ret_in[l])
        o_s, st_s = retention(p_s, 0, bs, ds, ds, pos_s, state_ret[l].astype(F32), True)
        o_meta, st_meta = retention(p_s, m_s, 1, N_META, N_META, pos_meta, zero_state, False)
        p_m = matmul(xn_m, w_ret_in[l])
        o_m, st_m = retention(p_m, 0, bp, seq, RET_BLOCK, pos_m, st_meta, False)
        ret_s.append(st_s.astype(state_ret.dtype))
        ret_p.append(st_m.astype(x_prompt.dtype))
        h_s, (xn_s,) = out_proj(jnp.concatenate([o_s, o_meta], axis=0), w_ret_out[l], h_s,
                                (norm_ffn[l],))
        h_m, (xn_m,) = out_proj(o_m, w_ret_out[l], h_m, (norm_ffn[l],))
        h_s, xns_s = channel(l, h_s, xn_s, next_gains(l))
        h_m, xns_m = channel(l, h_m, xn_m, next_gains(l))
        if l + 1 < n_a:
            (xn_s,), (xn_m,) = xns_s, xns_m

    (xkv_s, xq_s), (xkv_m, xq_m) = xns_s, xns_m
    k_sm, v_sm = kv_proj(xkv_s, w_kv, k_norm)
    k_m, v_m, kb, vt = kv_proj_blocks(xkv_m, w_kv, k_norm, bp, SB_BLOCK, 2)
    nkv = SB_KV_HEADS * SB_DH
    k_meta, v_meta = k_sm[m_s:], v_sm[m_s:]

    def extra_blocks(meta_rows):
        blk = jnp.pad(meta_rows.astype(BF16), ((SB_BLOCK - N_META, 0), (0, 0)))
        blk = blk.reshape(SB_BLOCK, SB_KV_HEADS, SB_DH).transpose(1, 0, 2)
        both = jnp.stack([blk, jnp.zeros_like(blk)], axis=1)
        return jnp.broadcast_to(both[None], (bp,) + both.shape)

    n_fb = seq // SB_BLOCK
    kb = lax.dynamic_update_slice(kb, extra_blocks(k_meta), (0, 0, n_fb, 0, 0))
    vt = lax.dynamic_update_slice(vt, extra_blocks(v_meta).transpose(0, 1, 2, 4, 3), (0, 0, n_fb, 0, 0))

    def with_meta(meta_rows, frames):
        meta_b = jnp.broadcast_to(meta_rows[None], (bp, N_META, nkv))
        return jnp.concatenate([meta_b, frames.reshape(bp, seq, nkv)], axis=1)

    k_p = with_meta(k_meta, k_m)
    v_p = with_meta(v_meta, v_m)
    k_s = k_sm[:m_s].reshape(bs, ds, nkv)
    v_s = v_sm[:m_s].reshape(bs, ds, nkv)
    cache_k2 = cache_k.reshape(bs, p0_s, nkv)
    cache_v2 = cache_v.reshape(bs, p0_s, nkv)

    h_s, xn_s = h_s[:m_s], xq_s[:m_s]
    xn_m = xq_m
    for l in range(n_a, depth):
        j = l - n_a
        q_s = q_proj(xn_s, w_sb_q[j], q_norm[j]).reshape(bs, ds, d)
        q_m = q_proj(xn_m, w_sb_q[j], q_norm[j]).reshape(bp, seq, d)
        o_s = sb_attention_sample(q_s, cache_k2, cache_v2, k_s.astype(cache_k.dtype),
                                  v_s.astype(cache_v.dtype), SB_BLOCK).reshape(m_s, d)
        o_m = sb_attention_prompt(q_m, kb, vt, SB_BLOCK).reshape(m_main, d)
        h_s, (xn_s,) = out_proj(o_s, w_sb_o[j], h_s, (norm_ffn[l],))
        h_m, (xn_m,) = out_proj(o_m, w_sb_o[j], h_m, (norm_ffn[l],))
        h_s, xns_s = channel(l, h_s, xn_s, next_gains(l))
        h_m, xns_m = channel(l, h_m, xn_m, next_gains(l))
        if l + 1 < depth:
            (xn_s,), (xn_m,) = xns_s, xns_m

    y_prompt = h_m.reshape(bp, seq, d)
    y_sample = h_s.reshape(bs, ds, d)
    kv4 = lambda t: t.reshape(t.shape[0], t.shape[1], SB_KV_HEADS, SB_DH)
    return (y_prompt, y_sample, jnp.stack(ret_p), kv4(k_p), kv4(v_p), jnp.stack(ret_s),
            kv4(k_s), kv4(v_s))
```

```python
import functools

import jax
import jax.numpy as jnp
from jax import lax
from jax.experimental import pallas as pl
from jax.experimental.pallas import tpu as pltpu

D_MODEL = 1024
N_META = 16
RET_BLOCK = 256
SB_BLOCK = 256
MOE_TILE = 1024
MOE_HALF = 512
MOE_CHUNK = 160
RET_HEADS = 4
RET_DK = 256
RET_DV = 512
RET_IN = RET_HEADS * (2 * RET_DK + 2 * RET_DV)
RET_THETA_BASE = 10000.0
SB_HEADS = 8
SB_KV_HEADS = 4
SB_GROUP = SB_HEADS // SB_KV_HEADS
SB_DH = 128
N_EXP = 8
EPS = 1e-6
LOG2E = 1.4426950408889634

LANES = 128
SUBLANES = 8
SB_RUN = SB_BLOCK // SUBLANES
SB_CHAIN = 4
BF16_ROWS = 16
VMEM_LIMIT = 56 * 1024 * 1024

F32 = jnp.float32
BF16 = jnp.bfloat16


def _params(n_axes):
    return pltpu.CompilerParams(dimension_semantics=("arbitrary",) * n_axes,
                                vmem_limit_bytes=VMEM_LIMIT)


def _tile(n, pref, mult=BF16_ROWS):
    if n <= pref:
        return n
    t = (pref // mult) * mult
    while t >= mult:
        if n % t == 0:
            return t
        t -= mult
    return n


def _dot(a, b):
    return jnp.dot(a, b, preferred_element_type=F32)


def _rms(x, g):
    return (x * lax.rsqrt(jnp.mean(x * x, axis=-1, keepdims=True) + EPS)) * g


def _rmsnorm_kernel(x_ref, g_ref, o_ref):
    o_ref[...] = _rms(x_ref[...], g_ref[...]).astype(o_ref.dtype)


def rmsnorm(x, g, tm_pref=1024):
    m, d = x.shape
    tm = _tile(m, tm_pref)
    return pl.pallas_call(
        _rmsnorm_kernel,
        grid=(m // tm,),
        in_specs=[pl.BlockSpec((tm, d), lambda i: (i, 0)),
                  pl.BlockSpec((1, d), lambda i: (0, 0))],
        out_specs=pl.BlockSpec((tm, d), lambda i: (i, 0)),
        out_shape=jax.ShapeDtypeStruct((m, d), BF16),
        compiler_params=_params(1),
        name="rmsnorm",
    )(x, g.reshape(1, d))


def _mm_kernel(x_ref, w_ref, o_ref):
    o_ref[...] = _dot(x_ref[...], w_ref[...]).astype(o_ref.dtype)


def matmul(x, w, out_dtype=F32, tm_pref=1024, tn_pref=1024):
    m, k = x.shape
    n = w.shape[1]
    tm = _tile(m, tm_pref)
    tn = _tile(n, tn_pref, LANES)
    return pl.pallas_call(
        _mm_kernel,
        grid=(n // tn, m // tm),
        in_specs=[pl.BlockSpec((tm, k), lambda j, i: (i, 0)),
                  pl.BlockSpec((k, tn), lambda j, i: (0, j))],
        out_specs=pl.BlockSpec((tm, tn), lambda j, i: (i, j)),
        out_shape=jax.ShapeDtypeStruct((m, n), out_dtype),
        compiler_params=_params(2),
        name="matmul",
    )(x, w)


def _swiglu_kernel(x_ref, wg_ref, wu_ref, o_ref):
    x = x_ref[...]
    gate = _dot(x, wg_ref[...])
    up = _dot(x, wu_ref[...])
    o_ref[...] = (gate * jax.nn.sigmoid(gate) * up).astype(o_ref.dtype)


def swiglu_in(x, w_in, tm_pref=512, tn_pref=1408):
    m, k = x.shape
    f = w_in.shape[1] // 2
    tm = _tile(m, tm_pref)
    tn = _tile(f, tn_pref, LANES)
    nj = f // tn
    return pl.pallas_call(
        _swiglu_kernel,
        grid=(nj, m // tm),
        in_specs=[pl.BlockSpec((tm, k), lambda j, i: (i, 0)),
                  pl.BlockSpec((k, tn), lambda j, i: (0, j)),
                  pl.BlockSpec((k, tn), lambda j, i: (0, j + nj))],
        out_specs=pl.BlockSpec((tm, tn), lambda j, i: (i, j)),
        out_shape=jax.ShapeDtypeStruct((m, f), BF16),
        compiler_params=_params(2),
        name="swiglu_in",
    )(x, w_in, w_in)


def _out_kernel(x_ref, w_ref, h_ref, *rest, n_norm):
    g_refs = rest[:n_norm]
    ho_ref = rest[n_norm]
    xn_refs = rest[n_norm + 1:]
    h = h_ref[...] + _dot(x_ref[...], w_ref[...])
    ho_ref[...] = h
    if n_norm:
        y = h * lax.rsqrt(jnp.mean(h * h, axis=-1, keepdims=True) + EPS)
        for g_ref, xn_ref in zip(g_refs, xn_refs):
            xn_ref[...] = (y * g_ref[...]).astype(xn_ref.dtype)


def out_proj(x, w, h, gains=(), tm_pref=512):
    m, k = x.shape
    d = w.shape[1]
    tm = _tile(m, tm_pref)
    n_norm = len(gains)
    row = pl.BlockSpec((tm, d), lambda i: (i, 0))
    gspec = pl.BlockSpec((1, d), lambda i: (0, 0))
    outs = pl.pallas_call(
        functools.partial(_out_kernel, n_norm=n_norm),
        grid=(m // tm,),
        in_specs=[pl.BlockSpec((tm, k), lambda i: (i, 0)),
                  pl.BlockSpec((k, d), lambda i: (0, 0)),
                  row] + [gspec] * n_norm,
        out_specs=[row] * (1 + n_norm),
        out_shape=[jax.ShapeDtypeStruct((m, d), F32)]
        + [jax.ShapeDtypeStruct((m, d), BF16)] * n_norm,
        compiler_params=_params(1),
        name="out_proj",
    )(x, w, h, *[g.reshape(1, d) for g in gains])
    return outs[0], tuple(outs[1:])


def _head_norm(y, g, n_heads):
    parts = []
    for hd in range(n_heads):
        yh = y[:, hd * SB_DH:(hd + 1) * SB_DH]
        parts.append(_rms(yh, g))
    return parts


def _qproj_kernel(x_ref, w_ref, g_ref, o_ref):
    y = _dot(x_ref[...], w_ref[...])
    g = g_ref[...]
    for hd, yh in enumerate(_head_norm(y, g, SB_HEADS)):
        o_ref[:, hd * SB_DH:(hd + 1) * SB_DH] = (yh * (SB_DH ** -0.5 * LOG2E)).astype(o_ref.dtype)


def q_proj(x, w, g, tm_pref=512):
    m, k = x.shape
    n = w.shape[1]
    tm = _tile(m, tm_pref)
    return pl.pallas_call(
        _qproj_kernel,
        grid=(m // tm,),
        in_specs=[pl.BlockSpec((tm, k), lambda i: (i, 0)),
                  pl.BlockSpec((k, n), lambda i: (0, 0)),
                  pl.BlockSpec((1, SB_DH), lambda i: (0, 0))],
        out_specs=pl.BlockSpec((tm, n), lambda i: (i, 0)),
        out_shape=jax.ShapeDtypeStruct((m, n), BF16),
        compiler_params=_params(1),
        name="q_proj",
    )(x, w, g.reshape(1, SB_DH))


def _kvproj_kernel(x_ref, w_ref, g_ref, k_ref, v_ref):
    y = _dot(x_ref[...], w_ref[...])
    nk = SB_KV_HEADS * SB_DH
    g = g_ref[...]
    for hd, yh in enumerate(_head_norm(y[:, :nk], g, SB_KV_HEADS)):
        k_ref[:, hd * SB_DH:(hd + 1) * SB_DH] = yh
    v_ref[...] = y[:, nk:]


def kv_proj(x, w, g, tm_pref=512):
    m, k = x.shape
    n = w.shape[1]
    nk = n // 2
    tm = _tile(m, tm_pref)
    return pl.pallas_call(
        _kvproj_kernel,
        grid=(m // tm,),
        in_specs=[pl.BlockSpec((tm, k), lambda i: (i, 0)),
                  pl.BlockSpec((k, n), lambda i: (0, 0)),
                  pl.BlockSpec((1, SB_DH), lambda i: (0, 0))],
        out_specs=[pl.BlockSpec((tm, nk), lambda i: (i, 0))] * 2,
        out_shape=[jax.ShapeDtypeStruct((m, nk), F32)] * 2,
        compiler_params=_params(1),
        name="kv_proj",
    )(x, w, g.reshape(1, SB_DH))


def _kvproj_blocks_kernel(x_ref, w_ref, g_ref, k_ref, v_ref, kb_ref, vt_ref, *, bk):
    y = _dot(x_ref[...], w_ref[...])
    nk = SB_KV_HEADS * SB_DH
    n_sub = x_ref.shape[0] // bk
    g = g_ref[...]
    for hd, yh in enumerate(_head_norm(y[:, :nk], g, SB_KV_HEADS)):
        k_ref[:, hd * SB_DH:(hd + 1) * SB_DH] = yh
        for r in range(n_sub):
            kb_ref[hd, r] = yh[r * bk:(r + 1) * bk].astype(BF16)
    v_ref[...] = y[:, nk:]
    for hd in range(SB_KV_HEADS):
        vh = y[:, nk + hd * SB_DH:nk + (hd + 1) * SB_DH]
        for r in range(n_sub):
            vt_ref[hd, r] = vh[r * bk:(r + 1) * bk].T.astype(BF16)


def kv_proj_blocks(x, w, g, batch, bk, n_extra, tm_pref=512):
    m, k = x.shape
    n = w.shape[1]
    nk = n // 2
    length = m // batch
    tm = tm_pref if length % tm_pref == 0 else bk
    assert length % tm == 0 and tm % bk == 0
    steps, n_sub = length // tm, tm // bk
    nblk = length // bk + n_extra
    return pl.pallas_call(
        functools.partial(_kvproj_blocks_kernel, bk=bk),
        grid=(batch, steps),
        in_specs=[pl.BlockSpec((tm, k), lambda b, i: (b * steps + i, 0)),
                  pl.BlockSpec((k, n), lambda b, i: (0, 0)),
                  pl.BlockSpec((1, SB_DH), lambda b, i: (0, 0))],
        out_specs=[pl.BlockSpec((tm, nk), lambda b, i: (b * steps + i, 0))] * 2
        + [pl.BlockSpec((None, SB_KV_HEADS, n_sub, bk, SB_DH), lambda b, i: (b, 0, i, 0, 0)),
           pl.BlockSpec((None, SB_KV_HEADS, n_sub, SB_DH, bk), lambda b, i: (b, 0, i, 0, 0))],
        out_shape=[jax.ShapeDtypeStruct((m, nk), F32)] * 2
        + [jax.ShapeDtypeStruct((batch, SB_KV_HEADS, nblk, bk, SB_DH), BF16),
           jax.ShapeDtypeStruct((batch, SB_KV_HEADS, nblk, SB_DH, bk), BF16)],
        compiler_params=_params(2),
        name="kv_proj_blocks",
    )(x, w, g.reshape(1, SB_DH))


def _rope(x, cos, sin):
    half = RET_DK // 2
    x1, x2 = x[:, :half], x[:, half:]
    return jnp.concatenate([x1 * cos - x2 * sin, x2 * cos + x1 * sin], axis=1)


def _ret_kernel(pq_ref, pk_ref, pv_ref, pg_ref, cos_ref, sin_ref, dmat_ref, qdec_ref, kdec_ref,
                sdec_ref, s0_ref, o_ref, sout_ref, s_scr, *, chunk, n_sub):
    step = pl.program_id(2)

    @pl.when(step == 0)
    def _():
        s_scr[...] = s0_ref[...]

    dmat = dmat_ref[...]
    qdec = qdec_ref[...]
    kdec = kdec_ref[...]
    sdec = sdec_ref[...]
    for c in range(n_sub):
        rows = pl.ds(c * chunk, chunk)
        cos = cos_ref[rows, :]
        sin = sin_ref[rows, :]
        q = _rope(pq_ref[rows, :], cos, sin)
        k = _rope(pk_ref[rows, :], cos, sin) * (RET_DK ** -0.5)
        v = pv_ref[rows, :].astype(BF16)
        s = s_scr[...]
        scores = lax.dot_general(q.astype(BF16), k.astype(BF16), (((1,), (1,)), ((), ())),
                                 preferred_element_type=F32) * dmat
        o = _dot(scores.astype(BF16), v) + _dot((q * qdec).astype(BF16), s.astype(BF16))
        kd_t = (k * kdec).T.astype(BF16)
        s_scr[...] = sdec * s + _dot(kd_t, v)
        o = o * lax.rsqrt(jnp.mean(o * o, axis=-1, keepdims=True) + EPS)
        g = pg_ref[rows, :]
        o_ref[rows, :] = (g * jax.nn.sigmoid(g) * o).astype(o_ref.dtype)

    @pl.when(step == pl.num_programs(2) - 1)
    def _():
        sout_ref[...] = s_scr[...]


def _ret_tables(chunk):
    lg = jnp.log(1.0 - 2.0 ** (-5.0 - jnp.arange(RET_HEADS, dtype=F32)))[:, None]
    i = jnp.arange(chunk, dtype=F32)
    diff = i[:, None] - i[None, :]
    dmat = jnp.where(diff >= 0, jnp.exp(jnp.maximum(diff, 0.0)[None] * lg[:, :, None]), 0.0)
    q_dec = jnp.exp((i + 1.0)[None] * lg)
    k_dec = jnp.exp((chunk - 1.0 - i)[None] * lg)
    s_dec = jnp.exp(chunk * lg)
    qdec = jnp.broadcast_to(q_dec[:, :, None], (RET_HEADS, chunk, RET_DK))
    kdec = jnp.broadcast_to(k_dec[:, :, None], (RET_HEADS, chunk, RET_DK))
    sdec = jnp.broadcast_to(s_dec[:, :, None], (RET_HEADS, 1, RET_DV))
    return dmat, qdec, kdec, sdec


def _rope_tables(pos):
    half = RET_DK // 2
    inv_freq = 1.0 / (RET_THETA_BASE ** jnp.linspace(0.0, 1.0, half, dtype=F32))
    ang = pos.astype(F32)[:, None] * inv_freq[None, :]
    return jnp.cos(ang), jnp.sin(ang)


def retention(p, row0, n_streams, length, chunk, pos, s0, per_stream_s0, rows_pref=512):
    n_sub = max(1, min(rows_pref // chunk, length // chunk))
    while (length // chunk) % n_sub:
        n_sub -= 1
    rows = n_sub * chunk
    steps = length // rows
    assert row0 % rows == 0
    base = row0 // rows
    cos, sin = _rope_tables(pos)
    dmat, qdec, kdec, sdec = _ret_tables(chunk)
    qb, vb = RET_HEADS, (2 * RET_HEADS * RET_DK) // RET_DV

    def prow(b, i):
        return base + b * steps + i

    s0_map = (lambda b, h, i: (b, h, 0, 0)) if per_stream_s0 else (lambda b, h, i: (0, h, 0, 0))
    return pl.pallas_call(
        functools.partial(_ret_kernel, chunk=chunk, n_sub=n_sub),
        grid=(n_streams, RET_HEADS, steps),
        in_specs=[
            pl.BlockSpec((rows, RET_DK), lambda b, h, i: (prow(b, i), h)),
            pl.BlockSpec((rows, RET_DK), lambda b, h, i: (prow(b, i), qb + h)),
            pl.BlockSpec((rows, RET_DV), lambda b, h, i: (prow(b, i), vb + h)),
            pl.BlockSpec((rows, RET_DV), lambda b, h, i: (prow(b, i), vb + RET_HEADS + h)),
            pl.BlockSpec((rows, RET_DK // 2), lambda b, h, i: (i, 0)),
            pl.BlockSpec((rows, RET_DK // 2), lambda b, h, i: (i, 0)),
            pl.BlockSpec((None, chunk, chunk), lambda b, h, i: (h, 0, 0)),
            pl.BlockSpec((None, chunk, RET_DK), lambda b, h, i: (h, 0, 0)),
            pl.BlockSpec((None, chunk, RET_DK), lambda b, h, i: (h, 0, 0)),
            pl.BlockSpec((None, 1, RET_DV), lambda b, h, i: (h, 0, 0)),
            pl.BlockSpec((None, None, RET_DK, RET_DV), s0_map),
        ],
        out_specs=[
            pl.BlockSpec((rows, RET_DV), lambda b, h, i: (b * steps + i, h)),
            pl.BlockSpec((None, None, RET_DK, RET_DV), lambda b, h, i: (b, h, 0, 0)),
        ],
        out_shape=[
            jax.ShapeDtypeStruct((n_streams * length, RET_HEADS * RET_DV), BF16),
            jax.ShapeDtypeStruct((n_streams, RET_HEADS, RET_DK, RET_DV), F32),
        ],
        scratch_shapes=[pltpu.VMEM((RET_DK, RET_DV), F32)],
        compiler_params=_params(3),
        name="retention",
    )(p, p, p, p, cos, sin, dmat, qdec, kdec, sdec, s0)


def _router_kernel(h_ref, g_ref, w_ref, b_ref, c_ref):
    xn = _rms(h_ref[...], g_ref[...])
    logits = jnp.dot(xn, w_ref[...], precision=lax.Precision.HIGHEST,
                     preferred_element_type=F32) + b_ref[...]
    lane = lax.broadcasted_iota(jnp.int32, logits.shape, 1).astype(F32)
    neg = jnp.float32(-jnp.inf)
    logits = jnp.where(lane < N_EXP, logits, neg)
    m1 = jnp.max(logits, axis=-1, keepdims=True)
    i1 = jnp.min(jnp.where(logits == m1, lane, float(LANES)), axis=-1, keepdims=True)
    rest = jnp.where(lane == i1, neg, logits)
    m2 = jnp.max(rest, axis=-1, keepdims=True)
    i2 = jnp.min(jnp.where(rest == m2, lane, float(LANES)), axis=-1, keepdims=True)
    e2 = jnp.exp(m2 - m1)
    den = 1.0 + e2
    c_ref[...] = jnp.where(lane == i1, 1.0 / den, 0.0) + jnp.where(lane == i2, e2 / den, 0.0)


def router(h, g, w_r, b_r, tm_pref=1024):
    m, d = h.shape
    tm = _tile(m, tm_pref)
    w_pad = jnp.zeros((d, LANES), F32).at[:, :N_EXP].set(w_r.astype(F32))
    b_pad = jnp.zeros((1, LANES), F32).at[0, :N_EXP].set(b_r.astype(F32))
    return pl.pallas_call(
        _router_kernel,
        grid=(m // tm,),
        in_specs=[pl.BlockSpec((tm, d), lambda i: (i, 0)),
                  pl.BlockSpec((1, d), lambda i: (0, 0)),
                  pl.BlockSpec((d, LANES), lambda i: (0, 0)),
                  pl.BlockSpec((1, LANES), lambda i: (0, 0))],
        out_specs=pl.BlockSpec((tm, LANES), lambda i: (i, 0)),
        out_shape=jax.ShapeDtypeStruct((m, LANES), F32),
        compiler_params=_params(1),
        name="router",
    )(h, g.reshape(1, d), w_pad, b_pad)


def _moe_kernel(x_ref, c_ref, wg_ref, wu_ref, wo_ref, h_ref, *rest, n_norm):
    g_refs = rest[:n_norm]
    ho_ref = rest[n_norm]
    xn_refs = rest[n_norm + 1:n_norm + 1 + n_norm]
    acc = rest[-1]
    e = pl.program_id(1)

    @pl.when(e == 0)
    def _():
        acc[...] = h_ref[...]

    comb = c_ref[...]
    lane = lax.broadcasted_iota(jnp.int32, comb.shape, 1)
    ce = jnp.sum(jnp.where(lane == e, comb, 0.0), axis=-1, keepdims=True)

    @pl.when(jnp.max(ce) > 0.0)
    def _():
        x = x_ref[...]
        gate = _dot(x, wg_ref[...])
        up = _dot(x, wu_ref[...])
        act = (gate * jax.nn.sigmoid(gate) * up).astype(BF16)
        acc[...] += ce * _dot(act, wo_ref[...])

    @pl.when(e == pl.num_programs(1) - 1)
    def _():
        h = acc[...]
        ho_ref[...] = h
        if n_norm:
            y = h * lax.rsqrt(jnp.mean(h * h, axis=-1, keepdims=True) + EPS)
            for g_ref, xn_ref in zip(g_refs, xn_refs):
                xn_ref[...] = (y * g_ref[...]).astype(xn_ref.dtype)


def moe(x, comb, w_in, w_out, h, gains=(), tm_pref=512):
    m, d = x.shape
    n_exp, _, f2 = w_in.shape
    f = f2 // 2
    tm = _tile(m, tm_pref)
    n_norm = len(gains)
    row = pl.BlockSpec((tm, d), lambda i, e: (i, 0))
    gspec = pl.BlockSpec((1, d), lambda i, e: (0, 0))
    outs = pl.pallas_call(
        functools.partial(_moe_kernel, n_norm=n_norm),
        grid=(m // tm, n_exp),
        in_specs=[row,
                  pl.BlockSpec((tm, LANES), lambda i, e: (i, 0)),
                  pl.BlockSpec((None, d, f), lambda i, e: (e, 0, 0)),
                  pl.BlockSpec((None, d, f), lambda i, e: (e, 0, 1)),
                  pl.BlockSpec((None, f, d), lambda i, e: (e, 0, 0)),
                  row] + [gspec] * n_norm,
        out_specs=[row] * (1 + n_norm),
        out_shape=[jax.ShapeDtypeStruct((m, d), F32)]
        + [jax.ShapeDtypeStruct((m, d), BF16)] * n_norm,
        scratch_shapes=[pltpu.VMEM((tm, d), F32)],
        compiler_params=_params(2),
        name="moe",
    )(x, comb, w_in, w_in, w_out, h, *[g.reshape(1, d) for g in gains])
    return outs[0], tuple(outs[1:])


def _moe_sparse_kernel(x_ref, c_ref, wg_ref, wu_ref, wo_ref, h_ref, *rest, n_norm, cap, half):
    g_refs = rest[:n_norm]
    ho_ref = rest[n_norm]
    xn_refs = rest[n_norm + 1:n_norm + 1 + n_norm]
    rankt_ref, combt_ref = rest[-2:]
    e = pl.program_id(1)
    n_half = x_ref.shape[0] // half

    @pl.when(e == 0)
    def _():
        ho_ref[...] = h_ref[...]
        r = lax.broadcasted_iota(jnp.int32, (half, half), 0)
        c = lax.broadcasted_iota(jnp.int32, (half, half), 1)
        before = jnp.where(c < r, 1.0, 0.0).astype(BF16)
        for hf in range(n_half):
            rows = pl.ds(hf * half, half)
            comb = c_ref[rows, :]
            sel = comb > 0.0
            rank = jnp.where(sel, _dot(before, jnp.where(sel, 1.0, 0.0).astype(BF16)), -1.0)
            rankt_ref[:, rows] = rank.T
            combt_ref[:, rows] = comb.T

    rk_row = rankt_ref[pl.ds(e, 1), :]
    gate_row = combt_ref[pl.ds(e, 1), :]
    n_routed = jnp.max(rk_row) + 1.0
    for ch in range(-(-half // cap)):
        @pl.when(n_routed > float(ch * cap))
        def _():
            i_col = (lax.broadcasted_iota(jnp.int32, (cap, 1), 0) + ch * cap).astype(F32)
            picks, xs, gcs = [], [], []
            for hf in range(n_half):
                cols = slice(hf * half, (hf + 1) * half)
                hit = rk_row[:, cols] == i_col
                pick = jnp.where(hit, 1.0, 0.0).astype(BF16)
                picks.append(pick)
                gcs.append(jnp.sum(jnp.where(hit, gate_row[:, cols], 0.0), axis=-1, keepdims=True))
                xs.append(_dot(pick, x_ref[pl.ds(hf * half, half), :]).astype(BF16))
            xs = jnp.concatenate(xs, axis=0)
            gate = _dot(xs, wg_ref[...])
            up = _dot(xs, wu_ref[...])
            act = (gate * jax.nn.sigmoid(gate) * up).astype(BF16)
            y = jnp.concatenate(gcs, axis=0) * _dot(act, wo_ref[...])
            hi = y.astype(BF16)
            lo = (y - hi.astype(F32)).astype(BF16)
            tn = (((0,), (0,)), ((), ()))
            for hf in range(n_half):
                sl = slice(hf * cap, (hf + 1) * cap)
                ho_ref[pl.ds(hf * half, half), :] += (
                    lax.dot_general(picks[hf], hi[sl], tn, preferred_element_type=F32)
                    + lax.dot_general(picks[hf], lo[sl], tn, preferred_element_type=F32))

    @pl.when(e == pl.num_programs(1) - 1)
    def _():
        if n_norm:
            h = ho_ref[...]
            y = h * lax.rsqrt(jnp.mean(h * h, axis=-1, keepdims=True) + EPS)
            for g_ref, xn_ref in zip(g_refs, xn_refs):
                xn_ref[...] = (y * g_ref[...]).astype(xn_ref.dtype)


def moe_sparse(x, comb, w_in, w_out, h, gains=(), tm=MOE_TILE, cap=MOE_CHUNK, half=MOE_HALF):
    m, d = x.shape
    n_exp, _, f2 = w_in.shape
    f = f2 // 2
    assert m % tm == 0 and tm % half == 0
    n_norm = len(gains)
    row = pl.BlockSpec((tm, d), lambda i, e: (i, 0))
    gspec = pl.BlockSpec((1, d), lambda i, e: (0, 0))
    outs = pl.pallas_call(
        functools.partial(_moe_sparse_kernel, n_norm=n_norm, cap=cap, half=half),
        grid=(m // tm, n_exp),
        in_specs=[row,
                  pl.BlockSpec((tm, LANES), lambda i, e: (i, 0)),
                  pl.BlockSpec((None, d, f), lambda i, e: (e, 0, 0)),
                  pl.BlockSpec((None, d, f), lambda i, e: (e, 0, 1)),
                  pl.BlockSpec((None, f, d), lambda i, e: (e, 0, 0)),
                  row] + [gspec] * n_norm,
        out_specs=[row] * (1 + n_norm),
        out_shape=[jax.ShapeDtypeStruct((m, d), F32)]
        + [jax.ShapeDtypeStruct((m, d), BF16)] * n_norm,
        scratch_shapes=[pltpu.VMEM((LANES, tm), F32), pltpu.VMEM((LANES, tm), F32)],
        compiler_params=_params(2),
        name="moe_sparse",
    )(x, comb, w_in, w_in, w_out, h, *[g.reshape(1, d) for g in gains])
    return outs[0], tuple(outs[1:])


def _softplus2(z):
    return jnp.maximum(jnp.log2(1.0 + jnp.exp2(jnp.minimum(z, 126.0))), z)


def _suffix_ones(n, keys_on_sublanes):
    r = lax.broadcasted_iota(jnp.int32, (n, n), 0)
    c = lax.broadcasted_iota(jnp.int32, (n, n), 1)
    return jnp.where((c >= r) if keys_on_sublanes else (r >= c), 1.0, 0.0).astype(BF16)


def _shift_up(x, k, sl):
    return jnp.where(sl < SUBLANES - k, pltpu.roll(x, SUBLANES - k, axis=0), 1.0)


def _sb_weights(z_ref, pre_ref, a_ref, slot, carry, valid_fn=None):
    rows = z_ref.shape[2]
    n_chain = SB_RUN // SB_CHAIN
    totals = []
    for c in range(n_chain):
        sub = jnp.ones((SUBLANES, rows), F32)
        for v in reversed(range(c * SB_CHAIN, (c + 1) * SB_CHAIN)):
            band = pl.ds(SUBLANES * v, SUBLANES)
            u = jnp.exp2(jnp.minimum(z_ref[slot, band, :], 126.0))
            miss = 1.0 / (1.0 + u)
            if valid_fn is not None:
                miss = jnp.where(valid_fn(v), miss, 1.0)
            nxt = sub * miss
            pre_ref[slot, band, :] = sub - nxt
            sub = nxt
        totals.append(sub)
    past = [None] * n_chain
    run = jnp.ones((SUBLANES, rows), F32)
    for c in reversed(range(n_chain)):
        past[c] = run
        run = run * totals[c]
    sl = lax.broadcasted_iota(jnp.int32, (SUBLANES, 1), 0)
    later = _shift_up(run, 1, sl)
    later = later * _shift_up(later, 1, sl)
    later = later * _shift_up(later, 2, sl)
    later = later * _shift_up(later, 4, sl)
    later = later * carry
    for c in range(n_chain):
        f = past[c] * later
        f2 = jnp.concatenate([f, f], axis=0)
        for v in range(c * SB_CHAIN, (c + 1) * SB_CHAIN, 2):
            band2 = pl.ds(SUBLANES * v, 2 * SUBLANES)
            a_ref[slot, band2, :] = (pre_ref[slot, band2, :] * f2).astype(BF16)
    return (later * run)[0:1, :]


def _sb_prompt_kernel(q_ref, k_ref, vt_ref, o_ref, acc_ref, z_buf, pre_buf, a_buf, *, tq, bk, n_frame_blocks):
    t = pl.program_id(2)
    rows = SB_GROUP * tq
    qs = jnp.concatenate([q_ref[:, g * SB_DH:(g + 1) * SB_DH] for g in range(SB_GROUP)], axis=0)
    qt = qs.T
    qfrm = t * tq + lax.rem(lax.broadcasted_iota(jnp.int32, (1, rows), 1), tq)
    per_tile = tq // bk
    n_before = t * per_tile
    sl = lax.broadcasted_iota(jnp.int32, (SUBLANES, 1), 0)

    def blk(u):
        return jnp.where(u < n_before, n_before - 1 - u,
                         jnp.minimum(n_frame_blocks + u - n_before, n_frame_blocks + 1))

    def stage_z(j, slot):
        z_buf[slot] = _dot(k_ref[j], qt)

    def stage_acc(j, slot):
        acc_ref[...] += _dot(vt_ref[j], a_buf[slot])

    assert per_tile < 3
    carry = jnp.ones((1, rows), F32)
    acc_ref[...] = jnp.zeros_like(acc_ref)
    diag = [n_before + per_tile - 1 - m for m in range(per_tile)]
    for m, j in enumerate(diag):
        stage_z(j, m)
    stage_z(blk(0), per_tile)
    for m, j in enumerate(diag):
        carry = _sb_weights(z_buf, pre_buf, a_buf, m, carry,
                            lambda v, j=j: (j * bk + SB_RUN * sl + v) < qfrm)
    for m, j in enumerate(diag):
        stage_acc(j, m)
    stage_z(blk(1), (per_tile + 1) % 3)
    carry = _sb_weights(z_buf, pre_buf, a_buf, per_tile, carry)

    def steady(i, carry):
        for r in range(3):
            stage_z(blk(3 * i + r + 2), (per_tile + r + 2) % 3)
            carry = _sb_weights(z_buf, pre_buf, a_buf, (per_tile + r + 1) % 3, carry)
            stage_acc(blk(3 * i + r), (per_tile + r) % 3)
        return carry

    lax.fori_loop(0, (n_before + 1 + 2) // 3, steady, carry)
    acc = acc_ref[...].T
    for g in range(SB_GROUP):
        o_ref[:, g * SB_DH:(g + 1) * SB_DH] = acc[g * tq:(g + 1) * tq].astype(o_ref.dtype)


def sb_attention_prompt(q, kb, vt, bk, tq_pref=256):
    b, lq, _ = q.shape
    nblk = kb.shape[2]
    tq = tq_pref if lq % tq_pref == 0 else bk
    assert lq % tq == 0 and tq % bk == 0 and nblk == lq // bk + 2 and bk == SUBLANES * SB_RUN
    kb = kb.reshape(b, SB_KV_HEADS, nblk, SUBLANES, SB_RUN, SB_DH).swapaxes(3, 4).reshape(kb.shape)
    vt = vt.reshape(b, SB_KV_HEADS, nblk, SB_DH, SUBLANES, SB_RUN).swapaxes(4, 5).reshape(vt.shape)
    gw = SB_GROUP * SB_DH
    rows = SB_GROUP * tq
    return pl.pallas_call(
        functools.partial(_sb_prompt_kernel, tq=tq, bk=bk, n_frame_blocks=lq // bk),
        grid=(b, SB_KV_HEADS, lq // tq),
        in_specs=[pl.BlockSpec((None, tq, gw), lambda bi, h, t: (bi, t, h)),
                  pl.BlockSpec((None, None, nblk, bk, SB_DH), lambda bi, h, t: (bi, h, 0, 0, 0)),
                  pl.BlockSpec((None, None, nblk, SB_DH, bk), lambda bi, h, t: (bi, h, 0, 0, 0))],
        out_specs=pl.BlockSpec((None, tq, gw), lambda bi, h, t: (bi, t, h)),
        out_shape=jax.ShapeDtypeStruct((b, lq, SB_HEADS * SB_DH), BF16),
        scratch_shapes=[pltpu.VMEM((SB_DH, rows), F32),
                        pltpu.VMEM((3, bk, rows), F32),
                        pltpu.VMEM((3, bk, rows), F32),
                        pltpu.VMEM((3, bk, rows), BF16)],
        compiler_params=_params(3),
        name="sb_attention_prompt",
    )(q, kb, vt)


def _sb_sample_kernel(q_ref, kc_ref, vc_ref, kn_ref, vn_ref, o_ref, *, bk):
    ds = q_ref.shape[0]
    n_cache = kc_ref.shape[0]
    rows = SB_GROUP * ds
    qs = jnp.concatenate([q_ref[:, g * SB_DH:(g + 1) * SB_DH] for g in range(SB_GROUP)], axis=0)
    qrow = lax.rem(lax.broadcasted_iota(jnp.int32, (rows, 1), 0), ds)
    new_valid = lax.broadcasted_iota(jnp.int32, (1, ds), 1) < qrow

    spans = [(kn_ref, vn_ref, 0, ds, new_valid)]
    whole = n_cache // bk
    if n_cache % bk:
        spans.append((kc_ref, vc_ref, whole * bk, n_cache % bk, None))
    spans += [(kc_ref, vc_ref, j * bk, bk, None) for j in reversed(range(whole))]

    zs = [lax.dot_general(qs, kr[st:st + sz, :].astype(BF16), (((1,), (1,)), ((), ())),
                          preferred_element_type=F32) for kr, _, st, sz, _ in spans]
    tris = {sz: _suffix_ones(sz, False) for sz in {sp[3] for sp in spans}}
    crun = jnp.zeros((rows, 1), F32)
    dds = []
    for z, (_, _, _, sz, valid) in zip(zs, spans):
        sp = _softplus2(z)
        if valid is not None:
            sp = jnp.where(valid, sp, 0.0)
        cum = _dot(sp.astype(BF16), tris[sz])
        dds.append(cum - crun)
        crun = crun - cum[:, 0:1]
    acc = jnp.zeros((rows, SB_DH), F32)
    for z, d, (_, vr, st, sz, valid) in zip(zs, dds, spans):
        a = jnp.exp2(z - d)
        if valid is not None:
            a = jnp.where(valid, a, 0.0)
        acc = acc + _dot(a.astype(BF16), vr[st:st + sz, :].astype(BF16))
    for g in range(SB_GROUP):
        o_ref[:, g * SB_DH:(g + 1) * SB_DH] = acc[g * ds:(g + 1) * ds].astype(o_ref.dtype)


def sb_attention_sample(q, cache_k, cache_v, k_new, v_new, bk=256):
    b, ds, _ = q.shape
    n_cache = cache_k.shape[1]
    gw = SB_GROUP * SB_DH
    cache = pl.BlockSpec((None, n_cache, SB_DH), lambda bi, h: (bi, 0, h))
    new = pl.BlockSpec((None, ds, SB_DH), lambda bi, h: (bi, 0, h))
    qo = pl.BlockSpec((None, ds, gw), lambda bi, h: (bi, 0, h))
    return pl.pallas_call(
        functools.partial(_sb_sample_kernel, bk=bk),
        grid=(b, SB_KV_HEADS),
        in_specs=[qo, cache, cache, new, new],
        out_specs=qo,
        out_shape=jax.ShapeDtypeStruct((b, ds, SB_HEADS * SB_DH), BF16),
        compiler_params=_params(2),
        name="sb_attention_sample",
    )(q, cache_k, cache_v, k_new, v_new)


def kernel(x_prompt, x_sample, state_ret, cache_k, cache_v, meta_tokens, norm_mix, norm_ffn, w_ret_in, w_ret_out, norm_kv, w_kv, k_norm, w_sb_q, q_norm, w_sb_o, w_ffn_in, w_ffn_out, w_router, b_router, w_exp_in, w_exp_out):
    bp, seq, d = x_prompt.shape
    bs, ds, _ = x_sample.shape
    n_a = w_ret_in.shape[0]
    depth = norm_mix.shape[0]
    p0_s = cache_k.shape[1]
    m_main, m_s = bp * seq, bs * ds

    w_ret_in, w_ret_out, w_kv, w_sb_q, w_sb_o, w_ffn_in, w_ffn_out, w_exp_in, w_exp_out = (
        w.astype(BF16) for w in (w_ret_in, w_ret_out, w_kv, w_sb_q, w_sb_o, w_ffn_in, w_ffn_out,
                                 w_exp_in, w_exp_out))

    def channel(l, h, xn, gains):
        if l % 2 == 0:
            act = swiglu_in(xn, w_ffn_in[l // 2])
            return out_proj(act, w_ffn_out[l // 2], h, gains)
        comb = router(h, norm_ffn[l], w_router[l // 2], b_router[l // 2])
        experts = moe_sparse if h.shape[0] % MOE_TILE == 0 else moe
        return experts(xn, comb, w_exp_in[l // 2], w_exp_out[l // 2], h, gains)

    def next_gains(l):
        if l + 1 == depth:
            return ()
        if l + 1 == n_a:
            return (norm_kv, norm_mix[l + 1])
        return (norm_mix[l + 1],)

    h_s = jnp.concatenate([x_sample.reshape(m_s, d), meta_tokens.astype(x_prompt.dtype)], axis=0)
    h_m = x_prompt.reshape(m_main, d)
    xn_s = rmsnorm(h_s, norm_mix[0])
    xn_m = rmsnorm(h_m, norm_mix[0])
    pos_meta = jnp.arange(N_META)
    pos_s = p0_s + jnp.arange(ds)
    pos_m = N_META + jnp.arange(seq)
    zero_state = jnp.zeros((1, RET_HEADS, RET_DK, RET_DV), F32)

    ret_p, ret_s = [], []
    for l in range(n_a):
        p_s = matmul(xn_s, w_ret_in[l])
        o_s, st_s = retention(p_s, 0, bs, ds, ds, pos_s, state_ret[l].astype(F32), True)
        o_meta, st_meta = retention(p_s, m_s, 1, N_META, N_META, pos_meta, zero_state, False)
        p_m = matmul(xn_m, w_ret_in[l])
        o_m, st_m = retention(p_m, 0, bp, seq, RET_BLOCK, pos_m, st_meta, False)
        ret_s.append(st_s.astype(state_ret.dtype))
        ret_p.append(st_m.astype(x_prompt.dtype))
        h_s, (xn_s,) = out_proj(jnp.concatenate([o_s, o_meta], axis=0), w_ret_out[l], h_s,
                                (norm_ffn[l],))
        h_m, (xn_m,) = out_proj(o_m, w_ret_out[l], h_m, (norm_ffn[l],))
        h_s, xns_s = channel(l, h_s, xn_s, next_gains(l))
        h_m, xns_m = channel(l, h_m, xn_m, next_gains(l))
        if l + 1 < n_a:
            (xn_s,), (xn_m,) = xns_s, xns_m

    (xkv_s, xq_s), (xkv_m, xq_m) = xns_s, xns_m
    k_sm, v_sm = kv_proj(xkv_s, w_kv, k_norm)
    k_m, v_m, kb, vt = kv_proj_blocks(xkv_m, w_kv, k_norm, bp, SB_BLOCK, 2)
    nkv = SB_KV_HEADS * SB_DH
    k_meta, v_meta = k_sm[m_s:], v_sm[m_s:]

    def extra_blocks(meta_rows):
        blk = jnp.pad(meta_rows.astype(BF16), ((SB_BLOCK - N_META, 0), (0, 0)))
        blk = blk.reshape(SB_BLOCK, SB_KV_HEADS, SB_DH).transpose(1, 0, 2)
        both = jnp.stack([blk, jnp.zeros_like(blk)], axis=1)
        return jnp.broadcast_to(both[None], (bp,) + both.shape)

    n_fb = seq // SB_BLOCK
    kb = lax.dynamic_update_slice(kb, extra_blocks(k_meta), (0, 0, n_fb, 0, 0))
    vt = lax.dynamic_update_slice(vt, extra_blocks(v_meta).transpose(0, 1, 2, 4, 3), (0, 0, n_fb, 0, 0))

    def with_meta(meta_rows, frames):
        meta_b = jnp.broadcast_to(meta_rows[None], (bp, N_META, nkv))
        return jnp.concatenate([meta_b, frames.reshape(bp, seq, nkv)], axis=1)

    k_p = with_meta(k_meta, k_m)
    v_p = with_meta(v_meta, v_m)
    k_s = k_sm[:m_s].reshape(bs, ds, nkv)
    v_s = v_sm[:m_s].reshape(bs, ds, nkv)
    cache_k2 = cache_k.reshape(bs, p0_s, nkv)
    cache_v2 = cache_v.reshape(bs, p0_s, nkv)

    h_s, xn_s = h_s[:m_s], xq_s[:m_s]
    xn_m = xq_m
    for l in range(n_a, depth):
        j = l - n_a
        q_s = q_proj(xn_s, w_sb_q[j], q_norm[j]).reshape(bs, ds, d)
        q_m = q_proj(xn_m, w_sb_q[j], q_norm[j]).reshape(bp, seq, d)
        o_s = sb_attention_sample(q_s, cache_k2, cache_v2, k_s.astype(cache_k.dtype),
                                  v_s.astype(cache_v.dtype), SB_BLOCK).reshape(m_s, d)
        o_m = sb_attention_prompt(q_m, kb, vt, SB_BLOCK).reshape(m_main, d)
        h_s, (xn_s,) = out_proj(o_s, w_sb_o[j], h_s, (norm_ffn[l],))
        h_m, (xn_m,) = out_proj(o_m, w_sb_o[j], h_m, (norm_ffn[l],))
        h_s, xns_s = channel(l, h_s, xn_s, next_gains(l))
        h_m, xns_m = channel(l, h_m, xn_m, next_gains(l))
        if l + 1 < depth:
            (xn_s,), (xn_m,) = xns_s, xns_m

    y_prompt = h_m.reshape(bp, seq, d)
    y_sample = h_s.reshape(bs, ds, d)
    kv4 = lambda t: t.reshape(t.shape[0], t.shape[1], SB_KV_HEADS, SB_DH)
    return (y_prompt, y_sample, jnp.stack(ret_p), kv4(k_p), kv4(v_p), jnp.stack(ret_s),
            kv4(k_s), kv4(v_s))
```

```python
import functools

import jax
import jax.numpy as jnp
from jax import lax
from jax.experimental import pallas as pl
from jax.experimental.pallas import tpu as pltpu

D_MODEL = 1024
N_META = 16
RET_BLOCK = 256
SB_BLOCK = 256
MOE_TILE = 1024
MOE_HALF = 512
MOE_CHUNK = 160
RET_HEADS = 4
RET_DK = 256
RET_DV = 512
RET_IN = RET_HEADS * (2 * RET_DK + 2 * RET_DV)
RET_THETA_BASE = 10000.0
SB_HEADS = 8
SB_KV_HEADS = 4
SB_GROUP = SB_HEADS // SB_KV_HEADS
SB_DH = 128
N_EXP = 8
EPS = 1e-6
LOG2E = 1.4426950408889634

LANES = 128
SUBLANES = 8
SB_RUN = SB_BLOCK // SUBLANES
SB_CHAIN = 4
BF16_ROWS = 16
VMEM_LIMIT = 56 * 1024 * 1024

F32 = jnp.float32
BF16 = jnp.bfloat16


def _params(n_axes):
    return pltpu.CompilerParams(dimension_semantics=("arbitrary",) * n_axes,
                                vmem_limit_bytes=VMEM_LIMIT)


def _tile(n, pref, mult=BF16_ROWS):
    if n <= pref:
        return n
    t = (pref // mult) * mult
    while t >= mult:
        if n % t == 0:
            return t
        t -= mult
    return n


def _dot(a, b):
    return jnp.dot(a, b, preferred_element_type=F32)


def _rms(x, g):
    return (x * lax.rsqrt(jnp.mean(x * x, axis=-1, keepdims=True) + EPS)) * g


def _rmsnorm_kernel(x_ref, g_ref, o_ref):
    o_ref[...] = _rms(x_ref[...], g_ref[...]).astype(o_ref.dtype)


def rmsnorm(x, g, tm_pref=1024):
    m, d = x.shape
    tm = _tile(m, tm_pref)
    return pl.pallas_call(
        _rmsnorm_kernel,
        grid=(m // tm,),
        in_specs=[pl.BlockSpec((tm, d), lambda i: (i, 0)),
                  pl.BlockSpec((1, d), lambda i: (0, 0))],
        out_specs=pl.BlockSpec((tm, d), lambda i: (i, 0)),
        out_shape=jax.ShapeDtypeStruct((m, d), BF16),
        compiler_params=_params(1),
        name="rmsnorm",
    )(x, g.reshape(1, d))


def _mm_kernel(x_ref, w_ref, o_ref):
    o_ref[...] = _dot(x_ref[...], w_ref[...]).astype(o_ref.dtype)


def matmul(x, w, out_dtype=F32, tm_pref=1024, tn_pref=1024):
    m, k = x.shape
    n = w.shape[1]
    tm = _tile(m, tm_pref)
    tn = _tile(n, tn_pref, LANES)
    return pl.pallas_call(
        _mm_kernel,
        grid=(n // tn, m // tm),
        in_specs=[pl.BlockSpec((tm, k), lambda j, i: (i, 0)),
                  pl.BlockSpec((k, tn), lambda j, i: (0, j))],
        out_specs=pl.BlockSpec((tm, tn), lambda j, i: (i, j)),
        out_shape=jax.ShapeDtypeStruct((m, n), out_dtype),
        compiler_params=_params(2),
        name="matmul",
    )(x, w)


def _swiglu_kernel(x_ref, wg_ref, wu_ref, o_ref):
    x = x_ref[...]
    gate = _dot(x, wg_ref[...])
    up = _dot(x, wu_ref[...])
    o_ref[...] = (gate * jax.nn.sigmoid(gate) * up).astype(o_ref.dtype)


def swiglu_in(x, w_in, tm_pref=512, tn_pref=1408):
    m, k = x.shape
    f = w_in.shape[1] // 2
    tm = _tile(m, tm_pref)
    tn = _tile(f, tn_pref, LANES)
    nj = f // tn
    return pl.pallas_call(
        _swiglu_kernel,
        grid=(nj, m // tm),
        in_specs=[pl.BlockSpec((tm, k), lambda j, i: (i, 0)),
                  pl.BlockSpec((k, tn), lambda j, i: (0, j)),
                  pl.BlockSpec((k, tn), lambda j, i: (0, j + nj))],
        out_specs=pl.BlockSpec((tm, tn), lambda j, i: (i, j)),
        out_shape=jax.ShapeDtypeStruct((m, f), BF16),
        compiler_params=_params(2),
        name="swiglu_in",
    )(x, w_in, w_in)


def _out_kernel(x_ref, w_ref, h_ref, *rest, n_norm):
    g_refs = rest[:n_norm]
    ho_ref = rest[n_norm]
    xn_refs = rest[n_norm + 1:]
    h = h_ref[...] + _dot(x_ref[...], w_ref[...])
    ho_ref[...] = h
    if n_norm:
        y = h * lax.rsqrt(jnp.mean(h * h, axis=-1, keepdims=True) + EPS)
        for g_ref, xn_ref in zip(g_refs, xn_refs):
            xn_ref[...] = (y * g_ref[...]).astype(xn_ref.dtype)


def out_proj(x, w, h, gains=(), tm_pref=512):
    m, k = x.shape
    d = w.shape[1]
    tm = _tile(m, tm_pref)
    n_norm = len(gains)
    row = pl.BlockSpec((tm, d), lambda i: (i, 0))
    gspec = pl.BlockSpec((1, d), lambda i: (0, 0))
    outs = pl.pallas_call(
        functools.partial(_out_kernel, n_norm=n_norm),
        grid=(m // tm,),
        in_specs=[pl.BlockSpec((tm, k), lambda i: (i, 0)),
                  pl.BlockSpec((k, d), lambda i: (0, 0)),
                  row] + [gspec] * n_norm,
        out_specs=[row] * (1 + n_norm),
        out_shape=[jax.ShapeDtypeStruct((m, d), F32)]
        + [jax.ShapeDtypeStruct((m, d), BF16)] * n_norm,
        compiler_params=_params(1),
        name="out_proj",
    )(x, w, h, *[g.reshape(1, d) for g in gains])
    return outs[0], tuple(outs[1:])


def _head_norm(y, g, n_heads):
    parts = []
    for hd in range(n_heads):
        yh = y[:, hd * SB_DH:(hd + 1) * SB_DH]
        parts.append(_rms(yh, g))
    return parts


def _qproj_kernel(x_ref, w_ref, g_ref, o_ref):
    y = _dot(x_ref[...], w_ref[...])
    g = g_ref[...]
    for hd, yh in enumerate(_head_norm(y, g, SB_HEADS)):
        o_ref[:, hd * SB_DH:(hd + 1) * SB_DH] = (yh * (SB_DH ** -0.5 * LOG2E)).astype(o_ref.dtype)


def q_proj(x, w, g, tm_pref=512):
    m, k = x.shape
    n = w.shape[1]
    tm = _tile(m, tm_pref)
    return pl.pallas_call(
        _qproj_kernel,
        grid=(m // tm,),
        in_specs=[pl.BlockSpec((tm, k), lambda i: (i, 0)),
                  pl.BlockSpec((k, n), lambda i: (0, 0)),
                  pl.BlockSpec((1, SB_DH), lambda i: (0, 0))],
        out_specs=pl.BlockSpec((tm, n), lambda i: (i, 0)),
        out_shape=jax.ShapeDtypeStruct((m, n), BF16),
        compiler_params=_params(1),
        name="q_proj",
    )(x, w, g.reshape(1, SB_DH))


def _kvproj_kernel(x_ref, w_ref, g_ref, k_ref, v_ref):
    y = _dot(x_ref[...], w_ref[...])
    nk = SB_KV_HEADS * SB_DH
    g = g_ref[...]
    for hd, yh in enumerate(_head_norm(y[:, :nk], g, SB_KV_HEADS)):
        k_ref[:, hd * SB_DH:(hd + 1) * SB_DH] = yh
    v_ref[...] = y[:, nk:]


def kv_proj(x, w, g, tm_pref=512):
    m, k = x.shape
    n = w.shape[1]
    nk = n // 2
    tm = _tile(m, tm_pref)
    return pl.pallas_call(
        _kvproj_kernel,
        grid=(m // tm,),
        in_specs=[pl.BlockSpec((tm, k), lambda i: (i, 0)),
                  pl.BlockSpec((k, n), lambda i: (0, 0)),
                  pl.BlockSpec((1, SB_DH), lambda i: (0, 0))],
        out_specs=[pl.BlockSpec((tm, nk), lambda i: (i, 0))] * 2,
        out_shape=[jax.ShapeDtypeStruct((m, nk), F32)] * 2,
        compiler_params=_params(1),
        name="kv_proj",
    )(x, w, g.reshape(1, SB_DH))


def _kvproj_blocks_kernel(x_ref, w_ref, g_ref, k_ref, v_ref, kb_ref, vt_ref, *, bk):
    y = _dot(x_ref[...], w_ref[...])
    nk = SB_KV_HEADS * SB_DH
    n_sub = x_ref.shape[0] // bk
    g = g_ref[...]
    for hd, yh in enumerate(_head_norm(y[:, :nk], g, SB_KV_HEADS)):
        k_ref[:, hd * SB_DH:(hd + 1) * SB_DH] = yh
        for r in range(n_sub):
            kb_ref[hd, r] = yh[r * bk:(r + 1) * bk].astype(BF16)
    v_ref[...] = y[:, nk:]
    for hd in range(SB_KV_HEADS):
        vh = y[:, nk + hd * SB_DH:nk + (hd + 1) * SB_DH]
        for r in range(n_sub):
            vt_ref[hd, r] = vh[r * bk:(r + 1) * bk].T.astype(BF16)


def kv_proj_blocks(x, w, g, batch, bk, n_extra, tm_pref=512):
    m, k = x.shape
    n = w.shape[1]
    nk = n // 2
    length = m // batch
    tm = tm_pref if length % tm_pref == 0 else bk
    assert length % tm == 0 and tm % bk == 0
    steps, n_sub = length // tm, tm // bk
    nblk = length // bk + n_extra
    return pl.pallas_call(
        functools.partial(_kvproj_blocks_kernel, bk=bk),
        grid=(batch, steps),
        in_specs=[pl.BlockSpec((tm, k), lambda b, i: (b * steps + i, 0)),
                  pl.BlockSpec((k, n), lambda b, i: (0, 0)),
                  pl.BlockSpec((1, SB_DH), lambda b, i: (0, 0))],
        out_specs=[pl.BlockSpec((tm, nk), lambda b, i: (b * steps + i, 0))] * 2
        + [pl.BlockSpec((None, SB_KV_HEADS, n_sub, bk, SB_DH), lambda b, i: (b, 0, i, 0, 0)),
           pl.BlockSpec((None, SB_KV_HEADS, n_sub, SB_DH, bk), lambda b, i: (b, 0, i, 0, 0))],
        out_shape=[jax.ShapeDtypeStruct((m, nk), F32)] * 2
        + [jax.ShapeDtypeStruct((batch, SB_KV_HEADS, nblk, bk, SB_DH), BF16),
           jax.ShapeDtypeStruct((batch, SB_KV_HEADS, nblk, SB_DH, bk), BF16)],
        compiler_params=_params(2),
        name="kv_proj_blocks",
    )(x, w, g.reshape(1, SB_DH))


def _rope(x, cos, sin):
    half = RET_DK // 2
    x1, x2 = x[:, :half], x[:, half:]
    return jnp.concatenate([x1 * cos - x2 * sin, x2 * cos + x1 * sin], axis=1)


def _ret_kernel(pq_ref, pk_ref, pv_ref, pg_ref, cos_ref, sin_ref, dmat_ref, qdec_ref, kdec_ref,
                sdec_ref, s0_ref, o_ref, sout_ref, s_scr, *, chunk, n_sub):
    step = pl.program_id(2)

    @pl.when(step == 0)
    def _():
        s_scr[...] = s0_ref[...]

    dmat = dmat_ref[...]
    qdec = qdec_ref[...]
    kdec = kdec_ref[...]
    sdec = sdec_ref[...]
    for c in range(n_sub):
        rows = pl.ds(c * chunk, chunk)
        cos = cos_ref[rows, :]
        sin = sin_ref[rows, :]
        q = _rope(pq_ref[rows, :].astype(F32), cos, sin)
        k = _rope(pk_ref[rows, :].astype(F32), cos, sin) * (RET_DK ** -0.5)
        v = pv_ref[rows, :].astype(BF16)
        s = s_scr[...]
        scores = lax.dot_general(q.astype(BF16), k.astype(BF16), (((1,), (1,)), ((), ())),
                                 preferred_element_type=F32) * dmat
        o = _dot(scores.astype(BF16), v) + _dot((q * qdec).astype(BF16), s.astype(BF16))
        kd_t = (k * kdec).T.astype(BF16)
        s_scr[...] = sdec * s + _dot(kd_t, v)
        o = o * lax.rsqrt(jnp.mean(o * o, axis=-1, keepdims=True) + EPS)
        g = pg_ref[rows, :].astype(F32)
        o_ref[rows, :] = (g * jax.nn.sigmoid(g) * o).astype(o_ref.dtype)

    @pl.when(step == pl.num_programs(2) - 1)
    def _():
        sout_ref[...] = s_scr[...]


def _ret_tables(chunk):
    lg = jnp.log(1.0 - 2.0 ** (-5.0 - jnp.arange(RET_HEADS, dtype=F32)))[:, None]
    i = jnp.arange(chunk, dtype=F32)
    diff = i[:, None] - i[None, :]
    dmat = jnp.where(diff >= 0, jnp.exp(jnp.maximum(diff, 0.0)[None] * lg[:, :, None]), 0.0)
    q_dec = jnp.exp((i + 1.0)[None] * lg)
    k_dec = jnp.exp((chunk - 1.0 - i)[None] * lg)
    s_dec = jnp.exp(chunk * lg)
    qdec = jnp.broadcast_to(q_dec[:, :, None], (RET_HEADS, chunk, RET_DK))
    kdec = jnp.broadcast_to(k_dec[:, :, None], (RET_HEADS, chunk, RET_DK))
    sdec = jnp.broadcast_to(s_dec[:, :, None], (RET_HEADS, 1, RET_DV))
    return dmat, qdec, kdec, sdec


def _rope_tables(pos):
    half = RET_DK // 2
    inv_freq = 1.0 / (RET_THETA_BASE ** jnp.linspace(0.0, 1.0, half, dtype=F32))
    ang = pos.astype(F32)[:, None] * inv_freq[None, :]
    return jnp.cos(ang), jnp.sin(ang)


def retention(p, row0, n_streams, length, chunk, pos, s0, per_stream_s0, rows_pref=512):
    n_sub = max(1, min(rows_pref // chunk, length // chunk))
    while (length // chunk) % n_sub:
        n_sub -= 1
    rows = n_sub * chunk
    steps = length // rows
    assert row0 % rows == 0
    base = row0 // rows
    cos, sin = _rope_tables(pos)
    dmat, qdec, kdec, sdec = _ret_tables(chunk)
    qb, vb = RET_HEADS, (2 * RET_HEADS * RET_DK) // RET_DV

    def prow(b, i):
        return base + b * steps + i

    s0_map = (lambda b, h, i: (b, h, 0, 0)) if per_stream_s0 else (lambda b, h, i: (0, h, 0, 0))
    return pl.pallas_call(
        functools.partial(_ret_kernel, chunk=chunk, n_sub=n_sub),
        grid=(n_streams, RET_HEADS, steps),
        in_specs=[
            pl.BlockSpec((rows, RET_DK), lambda b, h, i: (prow(b, i), h)),
            pl.BlockSpec((rows, RET_DK), lambda b, h, i: (prow(b, i), qb + h)),
            pl.BlockSpec((rows, RET_DV), lambda b, h, i: (prow(b, i), vb + h)),
            pl.BlockSpec((rows, RET_DV), lambda b, h, i: (prow(b, i), vb + RET_HEADS + h)),
            pl.BlockSpec((rows, RET_DK // 2), lambda b, h, i: (i, 0)),
            pl.BlockSpec((rows, RET_DK // 2), lambda b, h, i: (i, 0)),
            pl.BlockSpec((None, chunk, chunk), lambda b, h, i: (h, 0, 0)),
            pl.BlockSpec((None, chunk, RET_DK), lambda b, h, i: (h, 0, 0)),
            pl.BlockSpec((None, chunk, RET_DK), lambda b, h, i: (h, 0, 0)),
            pl.BlockSpec((None, 1, RET_DV), lambda b, h, i: (h, 0, 0)),
            pl.BlockSpec((None, None, RET_DK, RET_DV), s0_map),
        ],
        out_specs=[
            pl.BlockSpec((rows, RET_DV), lambda b, h, i: (b * steps + i, h)),
            pl.BlockSpec((None, None, RET_DK, RET_DV), lambda b, h, i: (b, h, 0, 0)),
        ],
        out_shape=[
            jax.ShapeDtypeStruct((n_streams * length, RET_HEADS * RET_DV), BF16),
            jax.ShapeDtypeStruct((n_streams, RET_HEADS, RET_DK, RET_DV), F32),
        ],
        scratch_shapes=[pltpu.VMEM((RET_DK, RET_DV), F32)],
        compiler_params=_params(3),
        name="retention",
    )(p, p, p, p, cos, sin, dmat, qdec, kdec, sdec, s0)


def _router_kernel(h_ref, g_ref, w_ref, b_ref, c_ref):
    xn = _rms(h_ref[...], g_ref[...])
    logits = jnp.dot(xn, w_ref[...], precision=lax.Precision.HIGHEST,
                     preferred_element_type=F32) + b_ref[...]
    lane = lax.broadcasted_iota(jnp.int32, logits.shape, 1).astype(F32)
    neg = jnp.float32(-jnp.inf)
    logits = jnp.where(lane < N_EXP, logits, neg)
    m1 = jnp.max(logits, axis=-1, keepdims=True)
    i1 = jnp.min(jnp.where(logits == m1, lane, float(LANES)), axis=-1, keepdims=True)
    rest = jnp.where(lane == i1, neg, logits)
    m2 = jnp.max(rest, axis=-1, keepdims=True)
    i2 = jnp.min(jnp.where(rest == m2, lane, float(LANES)), axis=-1, keepdims=True)
    e2 = jnp.exp(m2 - m1)
    den = 1.0 + e2
    c_ref[...] = jnp.where(lane == i1, 1.0 / den, 0.0) + jnp.where(lane == i2, e2 / den, 0.0)


def router(h, g, w_r, b_r, tm_pref=1024):
    m, d = h.shape
    tm = _tile(m, tm_pref)
    w_pad = jnp.zeros((d, LANES), F32).at[:, :N_EXP].set(w_r.astype(F32))
    b_pad = jnp.zeros((1, LANES), F32).at[0, :N_EXP].set(b_r.astype(F32))
    return pl.pallas_call(
        _router_kernel,
        grid=(m // tm,),
        in_specs=[pl.BlockSpec((tm, d), lambda i: (i, 0)),
                  pl.BlockSpec((1, d), lambda i: (0, 0)),
                  pl.BlockSpec((d, LANES), lambda i: (0, 0)),
                  pl.BlockSpec((1, LANES), lambda i: (0, 0))],
        out_specs=pl.BlockSpec((tm, LANES), lambda i: (i, 0)),
        out_shape=jax.ShapeDtypeStruct((m, LANES), F32),
        compiler_params=_params(1),
        name="router",
    )(h, g.reshape(1, d), w_pad, b_pad)


def _moe_kernel(x_ref, c_ref, wg_ref, wu_ref, wo_ref, h_ref, *rest, n_norm):
    g_refs = rest[:n_norm]
    ho_ref = rest[n_norm]
    xn_refs = rest[n_norm + 1:n_norm + 1 + n_norm]
    acc = rest[-1]
    e = pl.program_id(1)

    @pl.when(e == 0)
    def _():
        acc[...] = h_ref[...]

    comb = c_ref[...]
    lane = lax.broadcasted_iota(jnp.int32, comb.shape, 1)
    ce = jnp.sum(jnp.where(lane == e, comb, 0.0), axis=-1, keepdims=True)

    @pl.when(jnp.max(ce) > 0.0)
    def _():
        x = x_ref[...]
        gate = _dot(x, wg_ref[...])
        up = _dot(x, wu_ref[...])
        act = (gate * jax.nn.sigmoid(gate) * up).astype(BF16)
        acc[...] += ce * _dot(act, wo_ref[...])

    @pl.when(e == pl.num_programs(1) - 1)
    def _():
        h = acc[...]
        ho_ref[...] = h
        if n_norm:
            y = h * lax.rsqrt(jnp.mean(h * h, axis=-1, keepdims=True) + EPS)
            for g_ref, xn_ref in zip(g_refs, xn_refs):
                xn_ref[...] = (y * g_ref[...]).astype(xn_ref.dtype)


def moe(x, comb, w_in, w_out, h, gains=(), tm_pref=512):
    m, d = x.shape
    n_exp, _, f2 = w_in.shape
    f = f2 // 2
    tm = _tile(m, tm_pref)
    n_norm = len(gains)
    row = pl.BlockSpec((tm, d), lambda i, e: (i, 0))
    gspec = pl.BlockSpec((1, d), lambda i, e: (0, 0))
    outs = pl.pallas_call(
        functools.partial(_moe_kernel, n_norm=n_norm),
        grid=(m // tm, n_exp),
        in_specs=[row,
                  pl.BlockSpec((tm, LANES), lambda i, e: (i, 0)),
                  pl.BlockSpec((None, d, f), lambda i, e: (e, 0, 0)),
                  pl.BlockSpec((None, d, f), lambda i, e: (e, 0, 1)),
                  pl.BlockSpec((None, f, d), lambda i, e: (e, 0, 0)),
                  row] + [gspec] * n_norm,
        out_specs=[row] * (1 + n_norm),
        out_shape=[jax.ShapeDtypeStruct((m, d), F32)]
        + [jax.ShapeDtypeStruct((m, d), BF16)] * n_norm,
        scratch_shapes=[pltpu.VMEM((tm, d), F32)],
        compiler_params=_params(2),
        name="moe",
    )(x, comb, w_in, w_in, w_out, h, *[g.reshape(1, d) for g in gains])
    return outs[0], tuple(outs[1:])


def _moe_sparse_kernel(x_ref, c_ref, wg_ref, wu_ref, wo_ref, h_ref, *rest, n_norm, cap, half):
    g_refs = rest[:n_norm]
    ho_ref = rest[n_norm]
    xn_refs = rest[n_norm + 1:n_norm + 1 + n_norm]
    rankt_ref, combt_ref = rest[-2:]
    e = pl.program_id(1)
    n_half = x_ref.shape[0] // half

    @pl.when(e == 0)
    def _():
        ho_ref[...] = h_ref[...]
        r = lax.broadcasted_iota(jnp.int32, (half, half), 0)
        c = lax.broadcasted_iota(jnp.int32, (half, half), 1)
        before = jnp.where(c < r, 1.0, 0.0).astype(BF16)
        for hf in range(n_half):
            rows = pl.ds(hf * half, half)
            comb = c_ref[rows, :]
            sel = comb > 0.0
            rank = jnp.where(sel, _dot(before, jnp.where(sel, 1.0, 0.0).astype(BF16)), -1.0)
            rankt_ref[:, rows] = rank.T
            combt_ref[:, rows] = comb.T

    rk_row = rankt_ref[pl.ds(e, 1), :]
    gate_row = combt_ref[pl.ds(e, 1), :]
    n_routed = jnp.max(rk_row) + 1.0
    for ch in range(-(-half // cap)):
        @pl.when(n_routed > float(ch * cap))
        def _():
            i_col = (lax.broadcasted_iota(jnp.int32, (cap, 1), 0) + ch * cap).astype(F32)
            picks, xs, gcs = [], [], []
            for hf in range(n_half):
                cols = slice(hf * half, (hf + 1) * half)
                hit = rk_row[:, cols] == i_col
                pick = jnp.where(hit, 1.0, 0.0).astype(BF16)
                picks.append(pick)
                gcs.append(jnp.sum(jnp.where(hit, gate_row[:, cols], 0.0), axis=-1, keepdims=True))
                xs.append(_dot(pick, x_ref[pl.ds(hf * half, half), :]).astype(BF16))
            xs = jnp.concatenate(xs, axis=0)
            gate = _dot(xs, wg_ref[...])
            up = _dot(xs, wu_ref[...])
            act = (gate * jax.nn.sigmoid(gate) * up).astype(BF16)
            y = jnp.concatenate(gcs, axis=0) * _dot(act, wo_ref[...])
            yb = y.astype(BF16)
            tn = (((0,), (0,)), ((), ()))
            for hf in range(n_half):
                sl = slice(hf * cap, (hf + 1) * cap)
                ho_ref[pl.ds(hf * half, half), :] += lax.dot_general(
                    picks[hf], yb[sl], tn, preferred_element_type=F32)

    @pl.when(e == pl.num_programs(1) - 1)
    def _():
        if n_norm:
            h = ho_ref[...]
            y = h * lax.rsqrt(jnp.mean(h * h, axis=-1, keepdims=True) + EPS)
            for g_ref, xn_ref in zip(g_refs, xn_refs):
                xn_ref[...] = (y * g_ref[...]).astype(xn_ref.dtype)


def moe_sparse(x, comb, w_in, w_out, h, gains=(), tm=MOE_TILE, cap=MOE_CHUNK, half=MOE_HALF):
    m, d = x.shape
    n_exp, _, f2 = w_in.shape
    f = f2 // 2
    assert m % tm == 0 and tm % half == 0
    n_norm = len(gains)
    row = pl.BlockSpec((tm, d), lambda i, e: (i, 0))
    gspec = pl.BlockSpec((1, d), lambda i, e: (0, 0))
    outs = pl.pallas_call(
        functools.partial(_moe_sparse_kernel, n_norm=n_norm, cap=cap, half=half),
        grid=(m // tm, n_exp),
        in_specs=[row,
                  pl.BlockSpec((tm, LANES), lambda i, e: (i, 0)),
                  pl.BlockSpec((None, d, f), lambda i, e: (e, 0, 0)),
                  pl.BlockSpec((None, d, f), lambda i, e: (e, 0, 1)),
                  pl.BlockSpec((None, f, d), lambda i, e: (e, 0, 0)),
                  row] + [gspec] * n_norm,
        out_specs=[row] * (1 + n_norm),
        out_shape=[jax.ShapeDtypeStruct((m, d), F32)]
        + [jax.ShapeDtypeStruct((m, d), BF16)] * n_norm,
        scratch_shapes=[pltpu.VMEM((LANES, tm), F32), pltpu.VMEM((LANES, tm), F32)],
        compiler_params=_params(2),
        name="moe_sparse",
    )(x, comb, w_in, w_in, w_out, h, *[g.reshape(1, d) for g in gains])
    return outs[0], tuple(outs[1:])


def _softplus2(z):
    return jnp.maximum(jnp.log2(1.0 + jnp.exp2(jnp.minimum(z, 126.0))), z)


def _suffix_ones(n, keys_on_sublanes):
    r = lax.broadcasted_iota(jnp.int32, (n, n), 0)
    c = lax.broadcasted_iota(jnp.int32, (n, n), 1)
    return jnp.where((c >= r) if keys_on_sublanes else (r >= c), 1.0, 0.0).astype(BF16)


def _shift_up(x, k, sl):
    return jnp.where(sl < SUBLANES - k, pltpu.roll(x, SUBLANES - k, axis=0), 1.0)


def _sb_weights(z_ref, pre_ref, a_ref, slot, carry, valid_fn=None):
    rows = z_ref.shape[2]
    n_chain = SB_RUN // SB_CHAIN
    totals = []
    for c in range(n_chain):
        sub = jnp.ones((SUBLANES, rows), F32)
        for v in reversed(range(c * SB_CHAIN, (c + 1) * SB_CHAIN)):
            band = pl.ds(SUBLANES * v, SUBLANES)
            u = jnp.exp2(jnp.minimum(z_ref[slot, band, :], 126.0))
            miss = 1.0 / (1.0 + u)
            if valid_fn is not None:
                miss = jnp.where(valid_fn(v), miss, 1.0)
            nxt = sub * miss
            pre_ref[slot, band, :] = sub - nxt
            sub = nxt
        totals.append(sub)
    past = [None] * n_chain
    run = jnp.ones((SUBLANES, rows), F32)
    for c in reversed(range(n_chain)):
        past[c] = run
        run = run * totals[c]
    sl = lax.broadcasted_iota(jnp.int32, (SUBLANES, 1), 0)
    later = _shift_up(run, 1, sl)
    later = later * _shift_up(later, 1, sl)
    later = later * _shift_up(later, 2, sl)
    later = later * _shift_up(later, 4, sl)
    later = later * carry
    for c in range(n_chain):
        f = past[c] * later
        f2 = jnp.concatenate([f, f], axis=0)
        for v in range(c * SB_CHAIN, (c + 1) * SB_CHAIN, 2):
            band2 = pl.ds(SUBLANES * v, 2 * SUBLANES)
            a_ref[slot, band2, :] = (pre_ref[slot, band2, :] * f2).astype(BF16)
    return (later * run)[0:1, :]


def _sb_prompt_kernel(q_ref, k_ref, vt_ref, o_ref, acc_ref, z_buf, pre_buf, a_buf, *, tq, bk, n_frame_blocks):
    t = pl.program_id(2)
    rows = SB_GROUP * tq
    qs = jnp.concatenate([q_ref[:, g * SB_DH:(g + 1) * SB_DH] for g in range(SB_GROUP)], axis=0)
    qt = qs.T
    qfrm = t * tq + lax.rem(lax.broadcasted_iota(jnp.int32, (1, rows), 1), tq)
    per_tile = tq // bk
    n_before = t * per_tile
    sl = lax.broadcasted_iota(jnp.int32, (SUBLANES, 1), 0)

    def blk(u):
        return jnp.where(u < n_before, n_before - 1 - u,
                         jnp.minimum(n_frame_blocks + u - n_before, n_frame_blocks + 1))

    def stage_z(j, slot):
        z_buf[slot] = _dot(k_ref[j], qt)

    def stage_acc(j, slot):
        acc_ref[...] += _dot(vt_ref[j], a_buf[slot])

    assert per_tile < 3
    carry = jnp.ones((1, rows), F32)
    acc_ref[...] = jnp.zeros_like(acc_ref)
    diag = [n_before + per_tile - 1 - m for m in range(per_tile)]
    for m, j in enumerate(diag):
        stage_z(j, m)
    stage_z(blk(0), per_tile)
    for m, j in enumerate(diag):
        carry = _sb_weights(z_buf, pre_buf, a_buf, m, carry,
                            lambda v, j=j: (j * bk + SB_RUN * sl + v) < qfrm)
    for m, j in enumerate(diag):
        stage_acc(j, m)
    stage_z(blk(1), (per_tile + 1) % 3)
    carry = _sb_weights(z_buf, pre_buf, a_buf, per_tile, carry)

    def steady(i, carry):
        for r in range(3):
            stage_z(blk(3 * i + r + 2), (per_tile + r + 2) % 3)
            carry = _sb_weights(z_buf, pre_buf, a_buf, (per_tile + r + 1) % 3, carry)
            stage_acc(blk(3 * i + r), (per_tile + r) % 3)
        return carry

    lax.fori_loop(0, (n_before + 1 + 2) // 3, steady, carry)
    acc = acc_ref[...].T
    for g in range(SB_GROUP):
        o_ref[:, g * SB_DH:(g + 1) * SB_DH] = acc[g * tq:(g + 1) * tq].astype(o_ref.dtype)


def sb_attention_prompt(q, kb, vt, bk, tq_pref=256):
    b, lq, _ = q.shape
    nblk = kb.shape[2]
    tq = tq_pref if lq % tq_pref == 0 else bk
    assert lq % tq == 0 and tq % bk == 0 and nblk == lq // bk + 2 and bk == SUBLANES * SB_RUN
    kb = kb.reshape(b, SB_KV_HEADS, nblk, SUBLANES, SB_RUN, SB_DH).swapaxes(3, 4).reshape(kb.shape)
    vt = vt.reshape(b, SB_KV_HEADS, nblk, SB_DH, SUBLANES, SB_RUN).swapaxes(4, 5).reshape(vt.shape)
    gw = SB_GROUP * SB_DH
    rows = SB_GROUP * tq
    return pl.pallas_call(
        functools.partial(_sb_prompt_kernel, tq=tq, bk=bk, n_frame_blocks=lq // bk),
        grid=(b, SB_KV_HEADS, lq // tq),
        in_specs=[pl.BlockSpec((None, tq, gw), lambda bi, h, t: (bi, t, h)),
                  pl.BlockSpec((None, None, nblk, bk, SB_DH), lambda bi, h, t: (bi, h, 0, 0, 0)),
                  pl.BlockSpec((None, None, nblk, SB_DH, bk), lambda bi, h, t: (bi, h, 0, 0, 0))],
        out_specs=pl.BlockSpec((None, tq, gw), lambda bi, h, t: (bi, t, h)),
        out_shape=jax.ShapeDtypeStruct((b, lq, SB_HEADS * SB_DH), BF16),
        scratch_shapes=[pltpu.VMEM((SB_DH, rows), F32),
                        pltpu.VMEM((3, bk, rows), F32),
                        pltpu.VMEM((3, bk, rows), F32),
                        pltpu.VMEM((3, bk, rows), BF16)],
        compiler_params=_params(3),
        name="sb_attention_prompt",
    )(q, kb, vt)


def _sb_sample_kernel(q_ref, kc_ref, vc_ref, kn_ref, vn_ref, o_ref, *, bk):
    ds = q_ref.shape[0]
    n_cache = kc_ref.shape[0]
    rows = SB_GROUP * ds
    qs = jnp.concatenate([q_ref[:, g * SB_DH:(g + 1) * SB_DH] for g in range(SB_GROUP)], axis=0)
    qrow = lax.rem(lax.broadcasted_iota(jnp.int32, (rows, 1), 0), ds)
    new_valid = lax.broadcasted_iota(jnp.int32, (1, ds), 1) < qrow

    spans = [(kn_ref, vn_ref, 0, ds, new_valid)]
    whole = n_cache // bk
    if n_cache % bk:
        spans.append((kc_ref, vc_ref, whole * bk, n_cache % bk, None))
    spans += [(kc_ref, vc_ref, j * bk, bk, None) for j in reversed(range(whole))]

    zs = [lax.dot_general(qs, kr[st:st + sz, :].astype(BF16), (((1,), (1,)), ((), ())),
                          preferred_element_type=F32) for kr, _, st, sz, _ in spans]
    tris = {sz: _suffix_ones(sz, False) for sz in {sp[3] for sp in spans}}
    crun = jnp.zeros((rows, 1), F32)
    dds = []
    for z, (_, _, _, sz, valid) in zip(zs, spans):
        sp = _softplus2(z)
        if valid is not None:
            sp = jnp.where(valid, sp, 0.0)
        cum = _dot(sp.astype(BF16), tris[sz])
        dds.append(cum - crun)
        crun = crun - cum[:, 0:1]
    acc = jnp.zeros((rows, SB_DH), F32)
    for z, d, (_, vr, st, sz, valid) in zip(zs, dds, spans):
        a = jnp.exp2(z - d)
        if valid is not None:
            a = jnp.where(valid, a, 0.0)
        acc = acc + _dot(a.astype(BF16), vr[st:st + sz, :].astype(BF16))
    for g in range(SB_GROUP):
        o_ref[:, g * SB_DH:(g + 1) * SB_DH] = acc[g * ds:(g + 1) * ds].astype(o_ref.dtype)


def sb_attention_sample(q, cache_k, cache_v, k_new, v_new, bk=256):
    b, ds, _ = q.shape
    n_cache = cache_k.shape[1]
    gw = SB_GROUP * SB_DH
    cache = pl.BlockSpec((None, n_cache, SB_DH), lambda bi, h: (bi, 0, h))
    new = pl.BlockSpec((None, ds, SB_DH), lambda bi, h: (bi, 0, h))
    qo = pl.BlockSpec((None, ds, gw), lambda bi, h: (bi, 0, h))
    return pl.pallas_call(
        functools.partial(_sb_sample_kernel, bk=bk),
        grid=(b, SB_KV_HEADS),
        in_specs=[qo, cache, cache, new, new],
        out_specs=qo,
        out_shape=jax.ShapeDtypeStruct((b, ds, SB_HEADS * SB_DH), BF16),
        compiler_params=_params(2),
        name="sb_attention_sample",
    )(q, cache_k, cache_v, k_new, v_new)


def kernel(x_prompt, x_sample, state_ret, cache_k, cache_v, meta_tokens, norm_mix, norm_ffn, w_ret_in, w_ret_out, norm_kv, w_kv, k_norm, w_sb_q, q_norm, w_sb_o, w_ffn_in, w_ffn_out, w_router, b_router, w_exp_in, w_exp_out):
    bp, seq, d = x_prompt.shape
    bs, ds, _ = x_sample.shape
    n_a = w_ret_in.shape[0]
    depth = norm_mix.shape[0]
    p0_s = cache_k.shape[1]
    m_main, m_s = bp * seq, bs * ds

    w_ret_in, w_ret_out, w_kv, w_sb_q, w_sb_o, w_ffn_in, w_ffn_out, w_exp_in, w_exp_out = (
        w.astype(BF16) for w in (w_ret_in, w_ret_out, w_kv, w_sb_q, w_sb_o, w_ffn_in, w_ffn_out,
                                 w_exp_in, w_exp_out))

    def channel(l, h, xn, gains):
        if l % 2 == 0:
            act = swiglu_in(xn, w_ffn_in[l // 2])
            return out_proj(act, w_ffn_out[l // 2], h, gains)
        comb = router(h, norm_ffn[l], w_router[l // 2], b_router[l // 2])
        experts = moe_sparse if h.shape[0] % MOE_TILE == 0 else moe
        return experts(xn, comb, w_exp_in[l // 2], w_exp_out[l // 2], h, gains)

    def next_gains(l):
        if l + 1 == depth:
            return ()
        if l + 1 == n_a:
            return (norm_kv, norm_mix[l + 1])
        return (norm_mix[l + 1],)

    h_s = jnp.concatenate([x_sample.reshape(m_s, d), meta_tokens.astype(x_prompt.dtype)], axis=0)
    h_m = x_prompt.reshape(m_main, d)
    xn_s = rmsnorm(h_s, norm_mix[0])
    xn_m = rmsnorm(h_m, norm_mix[0])
    pos_meta = jnp.arange(N_META)
    pos_s = p0_s + jnp.arange(ds)
    pos_m = N_META + jnp.arange(seq)
    zero_state = jnp.zeros((1, RET_HEADS, RET_DK, RET_DV), F32)

    ret_p, ret_s = [], []
    for l in range(n_a):
        p_s = matmul(xn_s, w_ret_in[l])
        o_s, st_s = retention(p_s, 0, bs, ds, ds, pos_s, state_ret[l].astype(F32), True)
        o_meta, st_meta = retention(p_s, m_s, 1, N_META, N_META, pos_meta, zero_state, False)
        p_m = matmul(xn_m, w_ret_in[l], out_dtype=BF16)
        o_m, st_m = retention(p_m, 0, bp, seq, RET_BLOCK, pos_m, st_meta, False)
        ret_s.append(st_s.astype(state_ret.dtype))
        ret_p.append(st_m.astype(x_prompt.dtype))
        h_s, (xn_s,) = out_proj(jnp.concatenate([o_s, o_meta], axis=0), w_ret_out[l], h_s,
                                (norm_ffn[l],))
        h_m, (xn_m,) = out_proj(o_m, w_ret_out[l], h_m, (norm_ffn[l],))
        h_s, xns_s = channel(l, h_s, xn_s, next_gains(l))
        h_m, xns_m = channel(l, h_m, xn_m, next_gains(l))
        if l + 1 < n_a:
            (xn_s,), (xn_m,) = xns_s, xns_m

    (xkv_s, xq_s), (xkv_m, xq_m) = xns_s, xns_m
    k_sm, v_sm = kv_proj(xkv_s, w_kv, k_norm)
    k_m, v_m, kb, vt = kv_proj_blocks(xkv_m, w_kv, k_norm, bp, SB_BLOCK, 2)
    nkv = SB_KV_HEADS * SB_DH
    k_meta, v_meta = k_sm[m_s:], v_sm[m_s:]

    def extra_blocks(meta_rows):
        blk = jnp.pad(meta_rows.astype(BF16), ((SB_BLOCK - N_META, 0), (0, 0)))
        blk = blk.reshape(SB_BLOCK, SB_KV_HEADS, SB_DH).transpose(1, 0, 2)
        both = jnp.stack([blk, jnp.zeros_like(blk)], axis=1)
        return jnp.broadcast_to(both[None], (bp,) + both.shape)

    n_fb = seq // SB_BLOCK
    kb = lax.dynamic_update_slice(kb, extra_blocks(k_meta), (0, 0, n_fb, 0, 0))
    vt = lax.dynamic_update_slice(vt, extra_blocks(v_meta).transpose(0, 1, 2, 4, 3), (0, 0, n_fb, 0, 0))

    def with_meta(meta_rows, frames):
        meta_b = jnp.broadcast_to(meta_rows[None], (bp, N_META, nkv))
        return jnp.concatenate([meta_b, frames.reshape(bp, seq, nkv)], axis=1)

    k_p = with_meta(k_meta, k_m)
    v_p = with_meta(v_meta, v_m)
    k_s = k_sm[:m_s].reshape(bs, ds, nkv)
    v_s = v_sm[:m_s].reshape(bs, ds, nkv)
    cache_k2 = cache_k.reshape(bs, p0_s, nkv)
    cache_v2 = cache_v.reshape(bs, p0_s, nkv)

    h_s, xn_s = h_s[:m_s], xq_s[:m_s]
    xn_m = xq_m
    for l in range(n_a, depth):
        j = l - n_a
        q_s = q_proj(xn_s, w_sb_q[j], q_norm[j]).reshape(bs, ds, d)
        q_m = q_proj(xn_m, w_sb_q[j], q_norm[j]).reshape(bp, seq, d)
        o_s = sb_attention_sample(q_s, cache_k2, cache_v2, k_s.astype(cache_k.dtype),
                                  v_s.astype(cache_v.dtype), SB_BLOCK).reshape(m_s, d)
        o_m = sb_attention_prompt(q_m, kb, vt, SB_BLOCK).reshape(m_main, d)
        h_s, (xn_s,) = out_proj(o_s, w_sb_o[j], h_s, (norm_ffn[l],))
        h_m, (xn_m,) = out_proj(o_m, w_sb_o[j], h_m, (norm_ffn[l],))
        h_s, xns_s = channel(l, h_s, xn_s, next_gains(l))
        h_m, xns_m = channel(l, h_m, xn_m, next_gains(l))
        if l + 1 < depth:
            (xn_s,), (xn_m,) = xns_s, xns_m

    y_prompt = h_m.reshape(bp, seq, d)
    y_sample = h_s.reshape(bs, ds, d)
    kv4 = lambda t: t.reshape(t.shape[0], t.shape[1], SB_KV_HEADS, SB_DH)
    return (y_prompt, y_sample, jnp.stack(ret_p), kv4(k_p), kv4(v_p), jnp.stack(ret_s),
            kv4(k_s), kv4(v_s))
```

```python
import functools

import jax
import jax.numpy as jnp
from jax import lax
from jax.experimental import pallas as pl
from jax.experimental.pallas import tpu as pltpu

D_MODEL = 1024
N_META = 16
RET_BLOCK = 256
SB_BLOCK = 256
MOE_TILE = 1024
MOE_HALF = 512
MOE_CHUNK = 160
RET_HEADS = 4
RET_DK = 256
RET_DV = 512
RET_IN = RET_HEADS * (2 * RET_DK + 2 * RET_DV)
RET_THETA_BASE = 10000.0
SB_HEADS = 8
SB_KV_HEADS = 4
SB_GROUP = SB_HEADS // SB_KV_HEADS
SB_DH = 128
N_EXP = 8
EPS = 1e-6
LOG2E = 1.4426950408889634

LANES = 128
SUBLANES = 8
SB_RUN = SB_BLOCK // SUBLANES
SB_CHAIN = 4
BF16_ROWS = 16
VMEM_LIMIT = 56 * 1024 * 1024

F32 = jnp.float32
BF16 = jnp.bfloat16


def _params(n_axes):
    return pltpu.CompilerParams(dimension_semantics=("arbitrary",) * n_axes,
                                vmem_limit_bytes=VMEM_LIMIT)


def _tile(n, pref, mult=BF16_ROWS):
    if n <= pref:
        return n
    t = (pref // mult) * mult
    while t >= mult:
        if n % t == 0:
            return t
        t -= mult
    return n


def _dot(a, b):
    return jnp.dot(a, b, preferred_element_type=F32)


def _rms(x, g):
    return (x * lax.rsqrt(jnp.mean(x * x, axis=-1, keepdims=True) + EPS)) * g


def _rmsnorm_kernel(x_ref, g_ref, o_ref):
    o_ref[...] = _rms(x_ref[...], g_ref[...]).astype(o_ref.dtype)


def rmsnorm(x, g, tm_pref=1024):
    m, d = x.shape
    tm = _tile(m, tm_pref)
    return pl.pallas_call(
        _rmsnorm_kernel,
        grid=(m // tm,),
        in_specs=[pl.BlockSpec((tm, d), lambda i: (i, 0)),
                  pl.BlockSpec((1, d), lambda i: (0, 0))],
        out_specs=pl.BlockSpec((tm, d), lambda i: (i, 0)),
        out_shape=jax.ShapeDtypeStruct((m, d), BF16),
        compiler_params=_params(1),
        name="rmsnorm",
    )(x, g.reshape(1, d))


def _mm_kernel(x_ref, w_ref, o_ref):
    o_ref[...] = _dot(x_ref[...], w_ref[...]).astype(o_ref.dtype)


def matmul(x, w, out_dtype=F32, tm_pref=1024, tn_pref=1024):
    m, k = x.shape
    n = w.shape[1]
    tm = _tile(m, tm_pref)
    tn = _tile(n, tn_pref, LANES)
    return pl.pallas_call(
        _mm_kernel,
        grid=(n // tn, m // tm),
        in_specs=[pl.BlockSpec((tm, k), lambda j, i: (i, 0)),
                  pl.BlockSpec((k, tn), lambda j, i: (0, j))],
        out_specs=pl.BlockSpec((tm, tn), lambda j, i: (i, j)),
        out_shape=jax.ShapeDtypeStruct((m, n), out_dtype),
        compiler_params=_params(2),
        name="matmul",
    )(x, w)


def _swiglu_kernel(x_ref, wg_ref, wu_ref, o_ref):
    x = x_ref[...]
    gate = _dot(x, wg_ref[...])
    up = _dot(x, wu_ref[...])
    o_ref[...] = (gate * jax.nn.sigmoid(gate) * up).astype(o_ref.dtype)


def swiglu_in(x, w_in, tm_pref=512, tn_pref=1408):
    m, k = x.shape
    f = w_in.shape[1] // 2
    tm = _tile(m, tm_pref)
    tn = _tile(f, tn_pref, LANES)
    nj = f // tn
    return pl.pallas_call(
        _swiglu_kernel,
        grid=(nj, m // tm),
        in_specs=[pl.BlockSpec((tm, k), lambda j, i: (i, 0)),
                  pl.BlockSpec((k, tn), lambda j, i: (0, j)),
                  pl.BlockSpec((k, tn), lambda j, i: (0, j + nj))],
        out_specs=pl.BlockSpec((tm, tn), lambda j, i: (i, j)),
        out_shape=jax.ShapeDtypeStruct((m, f), BF16),
        compiler_params=_params(2),
        name="swiglu_in",
    )(x, w_in, w_in)


def _out_kernel(x_ref, w_ref, h_ref, *rest, n_norm):
    g_refs = rest[:n_norm]
    ho_ref = rest[n_norm]
    xn_refs = rest[n_norm + 1:]
    h = h_ref[...] + _dot(x_ref[...], w_ref[...])
    ho_ref[...] = h
    if n_norm:
        y = h * lax.rsqrt(jnp.mean(h * h, axis=-1, keepdims=True) + EPS)
        for g_ref, xn_ref in zip(g_refs, xn_refs):
            xn_ref[...] = (y * g_ref[...]).astype(xn_ref.dtype)


def out_proj(x, w, h, gains=(), tm_pref=512):
    m, k = x.shape
    d = w.shape[1]
    tm = _tile(m, tm_pref)
    n_norm = len(gains)
    row = pl.BlockSpec((tm, d), lambda i: (i, 0))
    gspec = pl.BlockSpec((1, d), lambda i: (0, 0))
    outs = pl.pallas_call(
        functools.partial(_out_kernel, n_norm=n_norm),
        grid=(m // tm,),
        in_specs=[pl.BlockSpec((tm, k), lambda i: (i, 0)),
                  pl.BlockSpec((k, d), lambda i: (0, 0)),
                  row] + [gspec] * n_norm,
        out_specs=[row] * (1 + n_norm),
        out_shape=[jax.ShapeDtypeStruct((m, d), F32)]
        + [jax.ShapeDtypeStruct((m, d), BF16)] * n_norm,
        compiler_params=_params(1),
        name="out_proj",
    )(x, w, h, *[g.reshape(1, d) for g in gains])
    return outs[0], tuple(outs[1:])


def _head_norm(y, g, n_heads):
    parts = []
    for hd in range(n_heads):
        yh = y[:, hd * SB_DH:(hd + 1) * SB_DH]
        parts.append(_rms(yh, g))
    return parts


def _qproj_kernel(x_ref, w_ref, g_ref, o_ref):
    y = _dot(x_ref[...], w_ref[...])
    g = g_ref[...]
    for hd, yh in enumerate(_head_norm(y, g, SB_HEADS)):
        o_ref[:, hd * SB_DH:(hd + 1) * SB_DH] = (yh * (SB_DH ** -0.5 * LOG2E)).astype(o_ref.dtype)


def q_proj(x, w, g, tm_pref=512):
    m, k = x.shape
    n = w.shape[1]
    tm = _tile(m, tm_pref)
    return pl.pallas_call(
        _qproj_kernel,
        grid=(m // tm,),
        in_specs=[pl.BlockSpec((tm, k), lambda i: (i, 0)),
                  pl.BlockSpec((k, n), lambda i: (0, 0)),
                  pl.BlockSpec((1, SB_DH), lambda i: (0, 0))],
        out_specs=pl.BlockSpec((tm, n), lambda i: (i, 0)),
        out_shape=jax.ShapeDtypeStruct((m, n), BF16),
        compiler_params=_params(1),
        name="q_proj",
    )(x, w, g.reshape(1, SB_DH))


def _kvproj_kernel(x_ref, w_ref, g_ref, k_ref, v_ref):
    y = _dot(x_ref[...], w_ref[...])
    nk = SB_KV_HEADS * SB_DH
    g = g_ref[...]
    for hd, yh in enumerate(_head_norm(y[:, :nk], g, SB_KV_HEADS)):
        k_ref[:, hd * SB_DH:(hd + 1) * SB_DH] = yh
    v_ref[...] = y[:, nk:]


def kv_proj(x, w, g, tm_pref=512):
    m, k = x.shape
    n = w.shape[1]
    nk = n // 2
    tm = _tile(m, tm_pref)
    return pl.pallas_call(
        _kvproj_kernel,
        grid=(m // tm,),
        in_specs=[pl.BlockSpec((tm, k), lambda i: (i, 0)),
                  pl.BlockSpec((k, n), lambda i: (0, 0)),
                  pl.BlockSpec((1, SB_DH), lambda i: (0, 0))],
        out_specs=[pl.BlockSpec((tm, nk), lambda i: (i, 0))] * 2,
        out_shape=[jax.ShapeDtypeStruct((m, nk), F32)] * 2,
        compiler_params=_params(1),
        name="kv_proj",
    )(x, w, g.reshape(1, SB_DH))


def _kvproj_blocks_kernel(x_ref, w_ref, g_ref, k_ref, v_ref, kb_ref, vt_ref, *, bk):
    y = _dot(x_ref[...], w_ref[...])
    nk = SB_KV_HEADS * SB_DH
    n_sub = x_ref.shape[0] // bk
    g = g_ref[...]
    for hd, yh in enumerate(_head_norm(y[:, :nk], g, SB_KV_HEADS)):
        k_ref[:, hd * SB_DH:(hd + 1) * SB_DH] = yh
        for r in range(n_sub):
            kb_ref[hd, r] = yh[r * bk:(r + 1) * bk].astype(BF16)
    v_ref[...] = y[:, nk:]
    for hd in range(SB_KV_HEADS):
        vh = y[:, nk + hd * SB_DH:nk + (hd + 1) * SB_DH]
        for r in range(n_sub):
            vt_ref[hd, r] = vh[r * bk:(r + 1) * bk].T.astype(BF16)


def kv_proj_blocks(x, w, g, batch, bk, n_extra, tm_pref=512):
    m, k = x.shape
    n = w.shape[1]
    nk = n // 2
    length = m // batch
    tm = tm_pref if length % tm_pref == 0 else bk
    assert length % tm == 0 and tm % bk == 0
    steps, n_sub = length // tm, tm // bk
    nblk = length // bk + n_extra
    return pl.pallas_call(
        functools.partial(_kvproj_blocks_kernel, bk=bk),
        grid=(batch, steps),
        in_specs=[pl.BlockSpec((tm, k), lambda b, i: (b * steps + i, 0)),
                  pl.BlockSpec((k, n), lambda b, i: (0, 0)),
                  pl.BlockSpec((1, SB_DH), lambda b, i: (0, 0))],
        out_specs=[pl.BlockSpec((tm, nk), lambda b, i: (b * steps + i, 0))] * 2
        + [pl.BlockSpec((None, SB_KV_HEADS, n_sub, bk, SB_DH), lambda b, i: (b, 0, i, 0, 0)),
           pl.BlockSpec((None, SB_KV_HEADS, n_sub, SB_DH, bk), lambda b, i: (b, 0, i, 0, 0))],
        out_shape=[jax.ShapeDtypeStruct((m, nk), F32)] * 2
        + [jax.ShapeDtypeStruct((batch, SB_KV_HEADS, nblk, bk, SB_DH), BF16),
           jax.ShapeDtypeStruct((batch, SB_KV_HEADS, nblk, SB_DH, bk), BF16)],
        compiler_params=_params(2),
        name="kv_proj_blocks",
    )(x, w, g.reshape(1, SB_DH))


def _rope(x, cos, sin):
    half = RET_DK // 2
    x1, x2 = x[:, :half], x[:, half:]
    return jnp.concatenate([x1 * cos - x2 * sin, x2 * cos + x1 * sin], axis=1)


def _ret_kernel(pq_ref, pk_ref, pv_ref, pg_ref, cos_ref, sin_ref, dmat_ref, qdec_ref, kdec_ref,
                sdec_ref, s0_ref, o_ref, sout_ref, s_scr, *, chunk, n_sub):
    step = pl.program_id(2)

    @pl.when(step == 0)
    def _():
        s_scr[...] = s0_ref[...]

    dmat = dmat_ref[...]
    qdec = qdec_ref[...]
    kdec = kdec_ref[...]
    sdec = sdec_ref[...]
    for c in range(n_sub):
        rows = pl.ds(c * chunk, chunk)
        cos = cos_ref[rows, :]
        sin = sin_ref[rows, :]
        q = _rope(pq_ref[rows, :].astype(F32), cos, sin)
        k = _rope(pk_ref[rows, :].astype(F32), cos, sin) * (RET_DK ** -0.5)
        v = pv_ref[rows, :].astype(BF16)
        s = s_scr[...]
        scores = lax.dot_general(q.astype(BF16), k.astype(BF16), (((1,), (1,)), ((), ())),
                                 preferred_element_type=F32) * dmat
        o = _dot(scores.astype(BF16), v) + _dot((q * qdec).astype(BF16), s.astype(BF16))
        kd_t = (k * kdec).T.astype(BF16)
        s_scr[...] = sdec * s + _dot(kd_t, v)
        o = o * lax.rsqrt(jnp.mean(o * o, axis=-1, keepdims=True) + EPS)
        g = pg_ref[rows, :].astype(F32)
        o_ref[rows, :] = (g * jax.nn.sigmoid(g) * o).astype(o_ref.dtype)

    @pl.when(step == pl.num_programs(2) - 1)
    def _():
        sout_ref[...] = s_scr[...]


def _ret_tables(chunk):
    lg = jnp.log(1.0 - 2.0 ** (-5.0 - jnp.arange(RET_HEADS, dtype=F32)))[:, None]
    i = jnp.arange(chunk, dtype=F32)
    diff = i[:, None] - i[None, :]
    dmat = jnp.where(diff >= 0, jnp.exp(jnp.maximum(diff, 0.0)[None] * lg[:, :, None]), 0.0)
    q_dec = jnp.exp((i + 1.0)[None] * lg)
    k_dec = jnp.exp((chunk - 1.0 - i)[None] * lg)
    s_dec = jnp.exp(chunk * lg)
    qdec = jnp.broadcast_to(q_dec[:, :, None], (RET_HEADS, chunk, RET_DK))
    kdec = jnp.broadcast_to(k_dec[:, :, None], (RET_HEADS, chunk, RET_DK))
    sdec = jnp.broadcast_to(s_dec[:, :, None], (RET_HEADS, 1, RET_DV))
    return dmat, qdec, kdec, sdec


def _rope_tables(pos):
    half = RET_DK // 2
    inv_freq = 1.0 / (RET_THETA_BASE ** jnp.linspace(0.0, 1.0, half, dtype=F32))
    ang = pos.astype(F32)[:, None] * inv_freq[None, :]
    return jnp.cos(ang), jnp.sin(ang)


def retention(p, row0, n_streams, length, chunk, pos, s0, per_stream_s0, rows_pref=512):
    n_sub = max(1, min(rows_pref // chunk, length // chunk))
    while (length // chunk) % n_sub:
        n_sub -= 1
    rows = n_sub * chunk
    steps = length // rows
    assert row0 % rows == 0
    base = row0 // rows
    cos, sin = _rope_tables(pos)
    dmat, qdec, kdec, sdec = _ret_tables(chunk)
    qb, vb = RET_HEADS, (2 * RET_HEADS * RET_DK) // RET_DV

    def prow(b, i):
        return base + b * steps + i

    s0_map = (lambda b, h, i: (b, h, 0, 0)) if per_stream_s0 else (lambda b, h, i: (0, h, 0, 0))
    return pl.pallas_call(
        functools.partial(_ret_kernel, chunk=chunk, n_sub=n_sub),
        grid=(n_streams, RET_HEADS, steps),
        in_specs=[
            pl.BlockSpec((rows, RET_DK), lambda b, h, i: (prow(b, i), h)),
            pl.BlockSpec((rows, RET_DK), lambda b, h, i: (prow(b, i), qb + h)),
            pl.BlockSpec((rows, RET_DV), lambda b, h, i: (prow(b, i), vb + h)),
            pl.BlockSpec((rows, RET_DV), lambda b, h, i: (prow(b, i), vb + RET_HEADS + h)),
            pl.BlockSpec((rows, RET_DK // 2), lambda b, h, i: (i, 0)),
            pl.BlockSpec((rows, RET_DK // 2), lambda b, h, i: (i, 0)),
            pl.BlockSpec((None, chunk, chunk), lambda b, h, i: (h, 0, 0)),
            pl.BlockSpec((None, chunk, RET_DK), lambda b, h, i: (h, 0, 0)),
            pl.BlockSpec((None, chunk, RET_DK), lambda b, h, i: (h, 0, 0)),
            pl.BlockSpec((None, 1, RET_DV), lambda b, h, i: (h, 0, 0)),
            pl.BlockSpec((None, None, RET_DK, RET_DV), s0_map),
        ],
        out_specs=[
            pl.BlockSpec((rows, RET_DV), lambda b, h, i: (b * steps + i, h)),
            pl.BlockSpec((None, None, RET_DK, RET_DV), lambda b, h, i: (b, h, 0, 0)),
        ],
        out_shape=[
            jax.ShapeDtypeStruct((n_streams * length, RET_HEADS * RET_DV), BF16),
            jax.ShapeDtypeStruct((n_streams, RET_HEADS, RET_DK, RET_DV), F32),
        ],
        scratch_shapes=[pltpu.VMEM((RET_DK, RET_DV), F32)],
        compiler_params=_params(3),
        name="retention",
    )(p, p, p, p, cos, sin, dmat, qdec, kdec, sdec, s0)


def _router_kernel(h_ref, g_ref, w_ref, b_ref, c_ref):
    xn = _rms(h_ref[...], g_ref[...])
    logits = jnp.dot(xn, w_ref[...], precision=lax.Precision.HIGHEST,
                     preferred_element_type=F32) + b_ref[...]
    lane = lax.broadcasted_iota(jnp.int32, logits.shape, 1).astype(F32)
    neg = jnp.float32(-jnp.inf)
    logits = jnp.where(lane < N_EXP, logits, neg)
    m1 = jnp.max(logits, axis=-1, keepdims=True)
    i1 = jnp.min(jnp.where(logits == m1, lane, float(LANES)), axis=-1, keepdims=True)
    rest = jnp.where(lane == i1, neg, logits)
    m2 = jnp.max(rest, axis=-1, keepdims=True)
    i2 = jnp.min(jnp.where(rest == m2, lane, float(LANES)), axis=-1, keepdims=True)
    e2 = jnp.exp(m2 - m1)
    den = 1.0 + e2
    c_ref[...] = jnp.where(lane == i1, 1.0 / den, 0.0) + jnp.where(lane == i2, e2 / den, 0.0)


def router(h, g, w_r, b_r, tm_pref=1024):
    m, d = h.shape
    tm = _tile(m, tm_pref)
    w_pad = jnp.zeros((d, LANES), F32).at[:, :N_EXP].set(w_r.astype(F32))
    b_pad = jnp.zeros((1, LANES), F32).at[0, :N_EXP].set(b_r.astype(F32))
    return pl.pallas_call(
        _router_kernel,
        grid=(m // tm,),
        in_specs=[pl.BlockSpec((tm, d), lambda i: (i, 0)),
                  pl.BlockSpec((1, d), lambda i: (0, 0)),
                  pl.BlockSpec((d, LANES), lambda i: (0, 0)),
                  pl.BlockSpec((1, LANES), lambda i: (0, 0))],
        out_specs=pl.BlockSpec((tm, LANES), lambda i: (i, 0)),
        out_shape=jax.ShapeDtypeStruct((m, LANES), F32),
        compiler_params=_params(1),
        name="router",
    )(h, g.reshape(1, d), w_pad, b_pad)


def _moe_kernel(x_ref, c_ref, wg_ref, wu_ref, wo_ref, h_ref, *rest, n_norm):
    g_refs = rest[:n_norm]
    ho_ref = rest[n_norm]
    xn_refs = rest[n_norm + 1:n_norm + 1 + n_norm]
    acc = rest[-1]
    e = pl.program_id(1)

    @pl.when(e == 0)
    def _():
        acc[...] = h_ref[...]

    comb = c_ref[...]
    lane = lax.broadcasted_iota(jnp.int32, comb.shape, 1)
    ce = jnp.sum(jnp.where(lane == e, comb, 0.0), axis=-1, keepdims=True)

    @pl.when(jnp.max(ce) > 0.0)
    def _():
        x = x_ref[...]
        gate = _dot(x, wg_ref[...])
        up = _dot(x, wu_ref[...])
        act = (gate * jax.nn.sigmoid(gate) * up).astype(BF16)
        acc[...] += ce * _dot(act, wo_ref[...])

    @pl.when(e == pl.num_programs(1) - 1)
    def _():
        h = acc[...]
        ho_ref[...] = h
        if n_norm:
            y = h * lax.rsqrt(jnp.mean(h * h, axis=-1, keepdims=True) + EPS)
            for g_ref, xn_ref in zip(g_refs, xn_refs):
                xn_ref[...] = (y * g_ref[...]).astype(xn_ref.dtype)


def moe(x, comb, w_in, w_out, h, gains=(), tm_pref=512):
    m, d = x.shape
    n_exp, _, f2 = w_in.shape
    f = f2 // 2
    tm = _tile(m, tm_pref)
    n_norm = len(gains)
    row = pl.BlockSpec((tm, d), lambda i, e: (i, 0))
    gspec = pl.BlockSpec((1, d), lambda i, e: (0, 0))
    outs = pl.pallas_call(
        functools.partial(_moe_kernel, n_norm=n_norm),
        grid=(m // tm, n_exp),
        in_specs=[row,
                  pl.BlockSpec((tm, LANES), lambda i, e: (i, 0)),
                  pl.BlockSpec((None, d, f), lambda i, e: (e, 0, 0)),
                  pl.BlockSpec((None, d, f), lambda i, e: (e, 0, 1)),
                  pl.BlockSpec((None, f, d), lambda i, e: (e, 0, 0)),
                  row] + [gspec] * n_norm,
        out_specs=[row] * (1 + n_norm),
        out_shape=[jax.ShapeDtypeStruct((m, d), F32)]
        + [jax.ShapeDtypeStruct((m, d), BF16)] * n_norm,
        scratch_shapes=[pltpu.VMEM((tm, d), F32)],
        compiler_params=_params(2),
        name="moe",
    )(x, comb, w_in, w_in, w_out, h, *[g.reshape(1, d) for g in gains])
    return outs[0], tuple(outs[1:])


def _moe_sparse_kernel(x_ref, c_ref, wg_ref, wu_ref, wo_ref, h_ref, *rest, n_norm, cap, half):
    g_refs = rest[:n_norm]
    ho_ref = rest[n_norm]
    xn_refs = rest[n_norm + 1:n_norm + 1 + n_norm]
    rankt_ref, combt_ref = rest[-2:]
    e = pl.program_id(1)
    n_half = x_ref.shape[0] // half

    @pl.when(e == 0)
    def _():
        ho_ref[...] = h_ref[...]
        r = lax.broadcasted_iota(jnp.int32, (half, half), 0)
        c = lax.broadcasted_iota(jnp.int32, (half, half), 1)
        before = jnp.where(c < r, 1.0, 0.0).astype(BF16)
        for hf in range(n_half):
            rows = pl.ds(hf * half, half)
            comb = c_ref[rows, :]
            sel = comb > 0.0
            rank = jnp.where(sel, _dot(before, jnp.where(sel, 1.0, 0.0).astype(BF16)), -1.0)
            rankt_ref[:, rows] = rank.T
            combt_ref[:, rows] = comb.T

    rk_row = rankt_ref[pl.ds(e, 1), :]
    gate_row = combt_ref[pl.ds(e, 1), :]
    n_routed = jnp.max(rk_row) + 1.0
    for ch in range(-(-half // cap)):
        @pl.when(n_routed > float(ch * cap))
        def _():
            i_col = (lax.broadcasted_iota(jnp.int32, (cap, 1), 0) + ch * cap).astype(F32)
            picks, xs, gcs = [], [], []
            for hf in range(n_half):
                cols = slice(hf * half, (hf + 1) * half)
                hit = rk_row[:, cols] == i_col
                pick = jnp.where(hit, 1.0, 0.0).astype(BF16)
                picks.append(pick)
                gcs.append(jnp.sum(jnp.where(hit, gate_row[:, cols], 0.0), axis=-1, keepdims=True))
                xs.append(_dot(pick, x_ref[pl.ds(hf * half, half), :]).astype(BF16))
            xs = jnp.concatenate(xs, axis=0)
            gate = _dot(xs, wg_ref[...])
            up = _dot(xs, wu_ref[...])
            act = (gate * jax.nn.sigmoid(gate) * up).astype(BF16)
            y = jnp.concatenate(gcs, axis=0) * _dot(act, wo_ref[...])
            yb = y.astype(BF16)
            tn = (((0,), (0,)), ((), ()))
            for hf in range(n_half):
                sl = slice(hf * cap, (hf + 1) * cap)
                ho_ref[pl.ds(hf * half, half), :] += lax.dot_general(
                    picks[hf], yb[sl], tn, preferred_element_type=F32)

    @pl.when(e == pl.num_programs(1) - 1)
    def _():
        if n_norm:
            h = ho_ref[...]
            y = h * lax.rsqrt(jnp.mean(h * h, axis=-1, keepdims=True) + EPS)
            for g_ref, xn_ref in zip(g_refs, xn_refs):
                xn_ref[...] = (y * g_ref[...]).astype(xn_ref.dtype)


def moe_sparse(x, comb, w_in, w_out, h, gains=(), tm=MOE_TILE, cap=MOE_CHUNK, half=MOE_HALF):
    m, d = x.shape
    n_exp, _, f2 = w_in.shape
    f = f2 // 2
    assert m % tm == 0 and tm % half == 0
    n_norm = len(gains)
    row = pl.BlockSpec((tm, d), lambda i, e: (i, 0))
    gspec = pl.BlockSpec((1, d), lambda i, e: (0, 0))
    outs = pl.pallas_call(
        functools.partial(_moe_sparse_kernel, n_norm=n_norm, cap=cap, half=half),
        grid=(m // tm, n_exp),
        in_specs=[row,
                  pl.BlockSpec((tm, LANES), lambda i, e: (i, 0)),
                  pl.BlockSpec((None, d, f), lambda i, e: (e, 0, 0)),
                  pl.BlockSpec((None, d, f), lambda i, e: (e, 0, 1)),
                  pl.BlockSpec((None, f, d), lambda i, e: (e, 0, 0)),
                  row] + [gspec] * n_norm,
        out_specs=[row] * (1 + n_norm),
        out_shape=[jax.ShapeDtypeStruct((m, d), F32)]
        + [jax.ShapeDtypeStruct((m, d), BF16)] * n_norm,
        scratch_shapes=[pltpu.VMEM((LANES, tm), F32), pltpu.VMEM((LANES, tm), F32)],
        compiler_params=_params(2),
        name="moe_sparse",
    )(x, comb, w_in, w_in, w_out, h, *[g.reshape(1, d) for g in gains])
    return outs[0], tuple(outs[1:])


def _softplus2(z):
    return jnp.maximum(jnp.log2(1.0 + jnp.exp2(jnp.minimum(z, 126.0))), z)


def _suffix_ones(n, keys_on_sublanes):
    r = lax.broadcasted_iota(jnp.int32, (n, n), 0)
    c = lax.broadcasted_iota(jnp.int32, (n, n), 1)
    return jnp.where((c >= r) if keys_on_sublanes else (r >= c), 1.0, 0.0).astype(BF16)


def _shift_up(x, k, sl):
    return jnp.where(sl < SUBLANES - k, pltpu.roll(x, SUBLANES - k, axis=0), 1.0)


def _sb_weights(z_ref, pre_ref, a_ref, slot, carry, valid_fn=None):
    rows = z_ref.shape[2]
    n_chain = SB_RUN // SB_CHAIN
    totals = []
    for c in range(n_chain):
        sub = jnp.ones((SUBLANES, rows), F32)
        for v in reversed(range(c * SB_CHAIN, (c + 1) * SB_CHAIN)):
            band = pl.ds(SUBLANES * v, SUBLANES)
            u = jnp.exp2(jnp.minimum(z_ref[slot, band, :], 126.0))
            miss = 1.0 / (1.0 + u)
            if valid_fn is not None:
                miss = jnp.where(valid_fn(v), miss, 1.0)
            nxt = sub * miss
            pre_ref[slot, band, :] = sub - nxt
            sub = nxt
        totals.append(sub)
    past = [None] * n_chain
    run = jnp.ones((SUBLANES, rows), F32)
    for c in reversed(range(n_chain)):
        past[c] = run
        run = run * totals[c]
    sl = lax.broadcasted_iota(jnp.int32, (SUBLANES, 1), 0)
    later = _shift_up(run, 1, sl)
    later = later * _shift_up(later, 1, sl)
    later = later * _shift_up(later, 2, sl)
    later = later * _shift_up(later, 4, sl)
    later = later * carry
    for c in range(n_chain):
        a_ref[slot, pl.ds(SUBLANES * c, SUBLANES), :] = past[c] * later
    return (later * run)[0:1, :]


def _sb_apply(pre_ref, f_ref, slot):
    rows_per_chain = SUBLANES * SB_CHAIN
    parts = []
    for c in range(SB_RUN // SB_CHAIN):
        f = f_ref[slot, pl.ds(SUBLANES * c, SUBLANES), :]
        fc = jnp.concatenate([f] * SB_CHAIN, axis=0)
        parts.append((pre_ref[slot, pl.ds(rows_per_chain * c, rows_per_chain), :] * fc).astype(BF16))
    return jnp.concatenate(parts, axis=0)


def _sb_prompt_kernel(q_ref, k_ref, vt_ref, o_ref, acc_ref, z_buf, pre_buf, a_buf, *, tq, bk, n_frame_blocks):
    t = pl.program_id(2)
    rows = SB_GROUP * tq
    qs = jnp.concatenate([q_ref[:, g * SB_DH:(g + 1) * SB_DH] for g in range(SB_GROUP)], axis=0)
    qt = qs.T
    qfrm = t * tq + lax.rem(lax.broadcasted_iota(jnp.int32, (1, rows), 1), tq)
    per_tile = tq // bk
    n_before = t * per_tile
    sl = lax.broadcasted_iota(jnp.int32, (SUBLANES, 1), 0)

    def blk(u):
        return jnp.where(u < n_before, n_before - 1 - u,
                         jnp.minimum(n_frame_blocks + u - n_before, n_frame_blocks + 1))

    def stage_z(j, slot):
        z_buf[slot] = _dot(k_ref[j], qt)

    def stage_acc(j, slot):
        acc_ref[...] += _dot(vt_ref[j], _sb_apply(pre_buf, a_buf, slot))

    assert per_tile < 3
    carry = jnp.ones((1, rows), F32)
    acc_ref[...] = jnp.zeros_like(acc_ref)
    diag = [n_before + per_tile - 1 - m for m in range(per_tile)]
    for m, j in enumerate(diag):
        stage_z(j, m)
    stage_z(blk(0), per_tile)
    for m, j in enumerate(diag):
        carry = _sb_weights(z_buf, pre_buf, a_buf, m, carry,
                            lambda v, j=j: (j * bk + SB_RUN * sl + v) < qfrm)
    for m, j in enumerate(diag):
        stage_acc(j, m)
    stage_z(blk(1), (per_tile + 1) % 3)
    carry = _sb_weights(z_buf, pre_buf, a_buf, per_tile, carry)

    def steady(i, carry):
        for r in range(3):
            stage_z(blk(3 * i + r + 2), (per_tile + r + 2) % 3)
            carry = _sb_weights(z_buf, pre_buf, a_buf, (per_tile + r + 1) % 3, carry)
            stage_acc(blk(3 * i + r), (per_tile + r) % 3)
        return carry

    lax.fori_loop(0, (n_before + 1 + 2) // 3, steady, carry)
    acc = acc_ref[...].T
    for g in range(SB_GROUP):
        o_ref[:, g * SB_DH:(g + 1) * SB_DH] = acc[g * tq:(g + 1) * tq].astype(o_ref.dtype)


def sb_attention_prompt(q, kb, vt, bk, tq_pref=256):
    b, lq, _ = q.shape
    nblk = kb.shape[2]
    tq = tq_pref if lq % tq_pref == 0 else bk
    assert lq % tq == 0 and tq % bk == 0 and nblk == lq // bk + 2 and bk == SUBLANES * SB_RUN
    kb = kb.reshape(b, SB_KV_HEADS, nblk, SUBLANES, SB_RUN, SB_DH).swapaxes(3, 4).reshape(kb.shape)
    vt = vt.reshape(b, SB_KV_HEADS, nblk, SB_DH, SUBLANES, SB_RUN).swapaxes(4, 5).reshape(vt.shape)
    gw = SB_GROUP * SB_DH
    rows = SB_GROUP * tq
    return pl.pallas_call(
        functools.partial(_sb_prompt_kernel, tq=tq, bk=bk, n_frame_blocks=lq // bk),
        grid=(b, SB_KV_HEADS, lq // tq),
        in_specs=[pl.BlockSpec((None, tq, gw), lambda bi, h, t: (bi, t, h)),
                  pl.BlockSpec((None, None, nblk, bk, SB_DH), lambda bi, h, t: (bi, h, 0, 0, 0)),
                  pl.BlockSpec((None, None, nblk, SB_DH, bk), lambda bi, h, t: (bi, h, 0, 0, 0))],
        out_specs=pl.BlockSpec((None, tq, gw), lambda bi, h, t: (bi, t, h)),
        out_shape=jax.ShapeDtypeStruct((b, lq, SB_HEADS * SB_DH), BF16),
        scratch_shapes=[pltpu.VMEM((SB_DH, rows), F32),
                        pltpu.VMEM((3, bk, rows), F32),
                        pltpu.VMEM((3, bk, rows), F32),
                        pltpu.VMEM((3, SUBLANES * (SB_RUN // SB_CHAIN), rows), F32)],
        compiler_params=_params(3),
        name="sb_attention_prompt",
    )(q, kb, vt)


def _sb_sample_kernel(q_ref, kc_ref, vc_ref, kn_ref, vn_ref, o_ref, *, bk):
    ds = q_ref.shape[0]
    n_cache = kc_ref.shape[0]
    rows = SB_GROUP * ds
    qs = jnp.concatenate([q_ref[:, g * SB_DH:(g + 1) * SB_DH] for g in range(SB_GROUP)], axis=0)
    qrow = lax.rem(lax.broadcasted_iota(jnp.int32, (rows, 1), 0), ds)
    new_valid = lax.broadcasted_iota(jnp.int32, (1, ds), 1) < qrow

    spans = [(kn_ref, vn_ref, 0, ds, new_valid)]
    whole = n_cache // bk
    if n_cache % bk:
        spans.append((kc_ref, vc_ref, whole * bk, n_cache % bk, None))
    spans += [(kc_ref, vc_ref, j * bk, bk, None) for j in reversed(range(whole))]

    zs = [lax.dot_general(qs, kr[st:st + sz, :].astype(BF16), (((1,), (1,)), ((), ())),
                          preferred_element_type=F32) for kr, _, st, sz, _ in spans]
    tris = {sz: _suffix_ones(sz, False) for sz in {sp[3] for sp in spans}}
    crun = jnp.zeros((rows, 1), F32)
    dds = []
    for z, (_, _, _, sz, valid) in zip(zs, spans):
        sp = _softplus2(z)
        if valid is not None:
            sp = jnp.where(valid, sp, 0.0)
        cum = _dot(sp.astype(BF16), tris[sz])
        dds.append(cum - crun)
        crun = crun - cum[:, 0:1]
    acc = jnp.zeros((rows, SB_DH), F32)
    for z, d, (_, vr, st, sz, valid) in zip(zs, dds, spans):
        a = jnp.exp2(z - d)
        if valid is not None:
            a = jnp.where(valid, a, 0.0)
        acc = acc + _dot(a.astype(BF16), vr[st:st + sz, :].astype(BF16))
    for g in range(SB_GROUP):
        o_ref[:, g * SB_DH:(g + 1) * SB_DH] = acc[g * ds:(g + 1) * ds].astype(o_ref.dtype)


def sb_attention_sample(q, cache_k, cache_v, k_new, v_new, bk=256):
    b, ds, _ = q.shape
    n_cache = cache_k.shape[1]
    gw = SB_GROUP * SB_DH
    cache = pl.BlockSpec((None, n_cache, SB_DH), lambda bi, h: (bi, 0, h))
    new = pl.BlockSpec((None, ds, SB_DH), lambda bi, h: (bi, 0, h))
    qo = pl.BlockSpec((None, ds, gw), lambda bi, h: (bi, 0, h))
    return pl.pallas_call(
        functools.partial(_sb_sample_kernel, bk=bk),
        grid=(b, SB_KV_HEADS),
        in_specs=[qo, cache, cache, new, new],
        out_specs=qo,
        out_shape=jax.ShapeDtypeStruct((b, ds, SB_HEADS * SB_DH), BF16),
        compiler_params=_params(2),
        name="sb_attention_sample",
    )(q, cache_k, cache_v, k_new, v_new)


def kernel(x_prompt, x_sample, state_ret, cache_k, cache_v, meta_tokens, norm_mix, norm_ffn, w_ret_in, w_ret_out, norm_kv, w_kv, k_norm, w_sb_q, q_norm, w_sb_o, w_ffn_in, w_ffn_out, w_router, b_router, w_exp_in, w_exp_out):
    bp, seq, d = x_prompt.shape
    bs, ds, _ = x_sample.shape
    n_a = w_ret_in.shape[0]
    depth = norm_mix.shape[0]
    p0_s = cache_k.shape[1]
    m_main, m_s = bp * seq, bs * ds

    w_ret_in, w_ret_out, w_kv, w_sb_q, w_sb_o, w_ffn_in, w_ffn_out, w_exp_in, w_exp_out = (
        w.astype(BF16) for w in (w_ret_in, w_ret_out, w_kv, w_sb_q, w_sb_o, w_ffn_in, w_ffn_out,
                                 w_exp_in, w_exp_out))

    def channel(l, h, xn, gains):
        if l % 2 == 0:
            act = swiglu_in(xn, w_ffn_in[l // 2])
            return out_proj(act, w_ffn_out[l // 2], h, gains)
        comb = router(h, norm_ffn[l], w_router[l // 2], b_router[l // 2])
        experts = moe_sparse if h.shape[0] % MOE_TILE == 0 else moe
        return experts(xn, comb, w_exp_in[l // 2], w_exp_out[l // 2], h, gains)

    def next_gains(l):
        if l + 1 == depth:
            return ()
        if l + 1 == n_a:
            return (norm_kv, norm_mix[l + 1])
        return (norm_mix[l + 1],)

    h_s = jnp.concatenate([x_sample.reshape(m_s, d), meta_tokens.astype(x_prompt.dtype)], axis=0)
    h_m = x_prompt.reshape(m_main, d)
    xn_s = rmsnorm(h_s, norm_mix[0])
    xn_m = rmsnorm(h_m, norm_mix[0])
    pos_meta = jnp.arange(N_META)
    pos_s = p0_s + jnp.arange(ds)
    pos_m = N_META + jnp.arange(seq)
    zero_state = jnp.zeros((1, RET_HEADS, RET_DK, RET_DV), F32)

    ret_p, ret_s = [], []
    for l in range(n_a):
        p_s = matmul(xn_s, w_ret_in[l])
        o_s, st_s = retention(p_s, 0, bs, ds, ds, pos_s, state_ret[l].astype(F32), True)
        o_meta, st_meta = retention(p_s, m_s, 1, N_META, N_META, pos_meta, zero_state, False)
        p_m = matmul(xn_m, w_ret_in[l], out_dtype=BF16)
        o_m, st_m = retention(p_m, 0, bp, seq, RET_BLOCK, pos_m, st_meta, False)
        ret_s.append(st_s.astype(state_ret.dtype))
        ret_p.append(st_m.astype(x_prompt.dtype))
        h_s, (xn_s,) = out_proj(jnp.concatenate([o_s, o_meta], axis=0), w_ret_out[l], h_s,
                                (norm_ffn[l],))
        h_m, (xn_m,) = out_proj(o_m, w_ret_out[l], h_m, (norm_ffn[l],))
        h_s, xns_s = channel(l, h_s, xn_s, next_gains(l))
        h_m, xns_m = channel(l, h_m, xn_m, next_gains(l))
        if l + 1 < n_a:
            (xn_s,), (xn_m,) = xns_s, xns_m

    (xkv_s, xq_s), (xkv_m, xq_m) = xns_s, xns_m
    k_sm, v_sm = kv_proj(xkv_s, w_kv, k_norm)
    k_m, v_m, kb, vt = kv_proj_blocks(xkv_m, w_kv, k_norm, bp, SB_BLOCK, 2)
    nkv = SB_KV_HEADS * SB_DH
    k_meta, v_meta = k_sm[m_s:], v_sm[m_s:]

    def extra_blocks(meta_rows):
        blk = jnp.pad(meta_rows.astype(BF16), ((SB_BLOCK - N_META, 0), (0, 0)))
        blk = blk.reshape(SB_BLOCK, SB_KV_HEADS, SB_DH).transpose(1, 0, 2)
        both = jnp.stack([blk, jnp.zeros_like(blk)], axis=1)
        return jnp.broadcast_to(both[None], (bp,) + both.shape)

    n_fb = seq // SB_BLOCK
    kb = lax.dynamic_update_slice(kb, extra_blocks(k_meta), (0, 0, n_fb, 0, 0))
    vt = lax.dynamic_update_slice(vt, extra_blocks(v_meta).transpose(0, 1, 2, 4, 3), (0, 0, n_fb, 0, 0))

    def with_meta(meta_rows, frames):
        meta_b = jnp.broadcast_to(meta_rows[None], (bp, N_META, nkv))
        return jnp.concatenate([meta_b, frames.reshape(bp, seq, nkv)], axis=1)

    k_p = with_meta(k_meta, k_m)
    v_p = with_meta(v_meta, v_m)
    k_s = k_sm[:m_s].reshape(bs, ds, nkv)
    v_s = v_sm[:m_s].reshape(bs, ds, nkv)
    cache_k2 = cache_k.reshape(bs, p0_s, nkv)
    cache_v2 = cache_v.reshape(bs, p0_s, nkv)

    h_s, xn_s = h_s[:m_s], xq_s[:m_s]
    xn_m = xq_m
    for l in range(n_a, depth):
        j = l - n_a
        q_s = q_proj(xn_s, w_sb_q[j], q_norm[j]).reshape(bs, ds, d)
        q_m = q_proj(xn_m, w_sb_q[j], q_norm[j]).reshape(bp, seq, d)
        o_s = sb_attention_sample(q_s, cache_k2, cache_v2, k_s.astype(cache_k.dtype),
                                  v_s.astype(cache_v.dtype), SB_BLOCK).reshape(m_s, d)
        o_m = sb_attention_prompt(q_m, kb, vt, SB_BLOCK).reshape(m_main, d)
        h_s, (xn_s,) = out_proj(o_s, w_sb_o[j], h_s, (norm_ffn[l],))
        h_m, (xn_m,) = out_proj(o_m, w_sb_o[j], h_m, (norm_ffn[l],))
        h_s, xns_s = channel(l, h_s, xn_s, next_gains(l))
        h_m, xns_m = channel(l, h_m, xn_m, next_gains(l))
        if l + 1 < depth:
            (xn_s,), (xn_m,) = xns_s, xns_m

    y_prompt = h_m.reshape(bp, seq, d)
    y_sample = h_s.reshape(bs, ds, d)
    kv4 = lambda t: t.reshape(t.shape[0], t.shape[1], SB_KV_HEADS, SB_DH)
    return (y_prompt, y_sample, jnp.stack(ret_p), kv4(k_p), kv4(v_p), jnp.stack(ret_s),
            kv4(k_s), kv4(v_s))
```
